```python
import math
import jax
import jax.numpy as jnp
from jax import lax
import numpy as np

D_MODEL = 2048
BATCH = 4
SEQ = 4096
DEPTH = 2

N_A_LAYERS = DEPTH // 2
N_B_LAYERS = DEPTH - N_A_LAYERS
EPS = 1e-6

SSM_EXPAND = 2
D_INNER = SSM_EXPAND * D_MODEL
SSM_HEAD_DIM = 64
SSM_HEADS = D_INNER // SSM_HEAD_DIM
SSM_GROUPS = 8
SSM_HEADS_PER_GROUP = SSM_HEADS // SSM_GROUPS
D_STATE = 128
CONV_WIDTH = 4
SSD_CHUNK = 128
D_BC = SSM_GROUPS * D_STATE
D_XBC = D_INNER + 2 * D_BC
D_IN_PROJ = D_INNER + D_XBC + SSM_HEADS

ATT_HEAD_DIM = 128
ATT_HEADS = D_MODEL // ATT_HEAD_DIM
D_ATT = ATT_HEADS * ATT_HEAD_DIM
Q_BLOCK = 128

FFN_MULT_OF = 256
D_FF = -(-8 * D_MODEL // (3 * FFN_MULT_OF)) * FFN_MULT_OF

kernel_name = 'yoco_mamba2_fox_hybrid'


def rmsnorm(x, w):
    xf = x.astype(jnp.float32)
    y = xf * lax.rsqrt(jnp.mean(xf * xf, axis=-1, keepdims=True) + EPS)
    return (y * w.astype(jnp.float32)).astype(x.dtype)


def causal_depthwise_conv(u, w, b):
    c = u.shape[-1]
    out = lax.conv_general_dilated(
        u, w[:, None, :].astype(u.dtype), window_strides=(1,),
        padding=[(CONV_WIDTH - 1, 0)], dimension_numbers=('NWC', 'WIO', 'NWC'),
        feature_group_count=c)
    return out + b.astype(u.dtype)


def ssd_chunked_scan(xdt, a, bm, cm):
    bsz, seq = xdt.shape[:2]
    nc = seq // SSD_CHUNK
    g, kh, p = SSM_GROUPS, SSM_HEADS_PER_GROUP, SSM_HEAD_DIM
    f32 = jnp.float32

    def chunks(t):
        t = t.reshape((bsz, nc, SSD_CHUNK) + t.shape[2:])
        return jnp.moveaxis(t, 1, 0)

    xc = chunks(xdt.astype(f32).reshape(bsz, seq, g, kh, p))
    ac = chunks(a.astype(f32).reshape(bsz, seq, g, kh))
    bc = chunks(bm.astype(f32))
    cc = chunks(cm.astype(f32))
    causal = jnp.tril(jnp.ones((SSD_CHUNK, SSD_CHUNK), dtype=bool))[None, :, :, None, None]

    def step(state, inp):
        x_, a_, b_, c_ = inp
        a_cum = jnp.cumsum(a_, axis=1)
        seg = a_cum[:, :, None] - a_cum[:, None, :]
        decay = jnp.exp(jnp.where(causal, seg, -jnp.inf))
        cb = jnp.einsum('btgn,bsgn->btsg', c_, b_)
        y_diag = jnp.einsum('btsg,btsgk,bsgkp->btgkp', cb, decay, x_)
        y_off = jnp.einsum('btgn,bgkpn->btgkp', c_, state) * jnp.exp(a_cum)[..., None]
        a_last = a_cum[:, -1]
        w_in = jnp.exp(a_last[:, None] - a_cum)
        new_state = state * jnp.exp(a_last)[..., None, None] + jnp.einsum('bsgn,bsgk,bsgkp->bgkpn', b_, w_in, x_)
        return new_state, y_diag + y_off

    state0 = jnp.zeros((bsz, g, kh, p, D_STATE), f32)
    _, y = lax.scan(step, state0, (xc, ac, bc, cc))
    return jnp.moveaxis(y, 0, 1).reshape(bsz, seq, SSM_HEADS, p)


def mamba2_mixer(h, in_proj, conv_w, conv_b, dt_bias, a_log, d_skip, gnorm_w, out_proj):
    bsz, seq, _ = h.shape
    f32 = jnp.float32
    zxbcdt = h @ in_proj
    z, xbc, dt_raw = jnp.split(zxbcdt, [D_INNER, D_INNER + D_XBC], axis=-1)
    xbc = jax.nn.silu(causal_depthwise_conv(xbc, conv_w, conv_b))
    xs, bm, cm = jnp.split(xbc, [D_INNER, D_INNER + D_BC], axis=-1)
    xs = xs.astype(f32).reshape(bsz, seq, SSM_HEADS, SSM_HEAD_DIM)
    bm = bm.reshape(bsz, seq, SSM_GROUPS, D_STATE)
    cm = cm.reshape(bsz, seq, SSM_GROUPS, D_STATE)
    dt = jax.nn.softplus(dt_raw.astype(f32) + dt_bias.astype(f32))
    a_neg = -jnp.exp(a_log.astype(f32))
    y = ssd_chunked_scan(xs * dt[..., None], dt * a_neg, bm, cm)
    y = y + xs * d_skip.astype(f32)[:, None]
    y = y.reshape(bsz, seq, D_INNER) * jax.nn.silu(z.astype(f32))
    yg = y.reshape(bsz, seq, SSM_GROUPS, D_INNER // SSM_GROUPS)
    yg = yg * lax.rsqrt(jnp.mean(yg * yg, axis=-1, keepdims=True) + EPS)
    y = (yg.reshape(bsz, seq, D_INNER) * gnorm_w.astype(f32)).astype(h.dtype)
    return y @ out_proj


def shared_kv(s, kv_norm_w, w_kvf, b_f, k_norm_w):
    bsz, seq, _ = s.shape
    kvf = rmsnorm(s, kv_norm_w) @ w_kvf
    k, v, f_logit = jnp.split(kvf, [D_ATT, 2 * D_ATT], axis=-1)
    k = rmsnorm(k.reshape(bsz, seq, ATT_HEADS, ATT_HEAD_DIM), k_norm_w)
    v = v.reshape(bsz, seq, ATT_HEADS, ATT_HEAD_DIM)
    log_f = jax.nn.log_sigmoid(f_logit.astype(jnp.float32) + b_f.astype(jnp.float32))
    cum = jnp.cumsum(log_f, axis=1)
    return k, v, cum


def forgetting_attention(h, k, v, cum, w_q, q_norm_w, w_o):
    bsz, seq, _ = h.shape
    q = rmsnorm((h @ w_q).reshape(bsz, seq, ATT_HEADS, ATT_HEAD_DIM), q_norm_w)
    scale = ATT_HEAD_DIM ** -0.5
    cum_h = jnp.swapaxes(cum, 1, 2)
    outs = []
    for blk in range(seq // Q_BLOCK):
        q0 = blk * Q_BLOCK
        kend = q0 + Q_BLOCK
        logits = jnp.einsum('bthd,bshd->bhts', q[:, q0:kend], k[:, :kend]).astype(jnp.float32) * scale
        logits = logits + (cum_h[:, :, q0:kend, None] - cum_h[:, :, None, :kend])
        mask = jnp.arange(kend)[None, :] <= (q0 + jnp.arange(Q_BLOCK))[:, None]
        logits = jnp.where(mask, logits, -jnp.inf)
        probs = jax.nn.softmax(logits, axis=-1).astype(v.dtype)
        outs.append(jnp.einsum('bhts,bshd->bthd', probs, v[:, :kend]))
    o = jnp.concatenate(outs, axis=1).reshape(bsz, seq, D_ATT)
    return o @ w_o


def swiglu(h, w_gate_up, w_down):
    g, u = jnp.split(h @ w_gate_up, 2, axis=-1)
    return (jax.nn.silu(g) * u) @ w_down


def setup_inputs(seed: int = 0) -> dict:
    key = jax.random.key(seed)
    ks = jax.random.split(key, 24)
    f32 = jnp.float32

    def nrm(k, shape, scale):
        return jax.random.normal(k, shape, f32) * scale

    def gain(k, shape):
        return 1.0 + 0.02 * jax.random.normal(k, shape, f32)

    x = nrm(ks[0], (BATCH, SEQ, D_MODEL), 1.0)
    a_norm_w = gain(ks[1], (N_A_LAYERS, D_MODEL))
    a_in_proj = nrm(ks[2], (N_A_LAYERS, D_MODEL, D_IN_PROJ), D_MODEL ** -0.5)
    a_conv_w = nrm(ks[3], (N_A_LAYERS, CONV_WIDTH, D_XBC), CONV_WIDTH ** -0.5)
    a_conv_b = nrm(ks[4], (N_A_LAYERS, D_XBC), 0.02)
    dt0 = jnp.exp(jax.random.uniform(ks[5], (N_A_LAYERS, SSM_HEADS), f32, math.log(1e-3), math.log(1e-1)))
    a_dt_bias = dt0 + jnp.log(-jnp.expm1(-dt0))
    a_A_log = jnp.log(jax.random.uniform(ks[6], (N_A_LAYERS, SSM_HEADS), f32, 1.0, 16.0))
    a_D = 1.0 + 0.1 * jax.random.normal(ks[7], (N_A_LAYERS, SSM_HEADS), f32)
    a_gnorm_w = gain(ks[8], (N_A_LAYERS, D_INNER))
    a_out_proj = nrm(ks[9], (N_A_LAYERS, D_INNER, D_MODEL), D_INNER ** -0.5)
    kv_norm_w = gain(ks[10], (D_MODEL,))
    w_kvf = nrm(ks[11], (D_MODEL, 2 * D_ATT + ATT_HEADS), D_MODEL ** -0.5)
    b_f = jax.random.uniform(ks[12], (ATT_HEADS,), f32, 1.0, 4.0)
    k_norm_w = gain(ks[13], (ATT_HEAD_DIM,))
    b_norm_w = gain(ks[14], (N_B_LAYERS, D_MODEL))
    w_q = nrm(ks[15], (N_B_LAYERS, D_MODEL, D_ATT), D_MODEL ** -0.5)
    q_norm_w = gain(ks[16], (N_B_LAYERS, ATT_HEAD_DIM))
    w_o = nrm(ks[17], (N_B_LAYERS, D_ATT, D_MODEL), D_ATT ** -0.5)
    ffn_norm_w = gain(ks[18], (DEPTH, D_MODEL))
    w_gate_up = nrm(ks[19], (DEPTH, D_MODEL, 2 * D_FF), D_MODEL ** -0.5)
    w_down = nrm(ks[20], (DEPTH, D_FF, D_MODEL), D_FF ** -0.5)
    return {'x': x, 'a_norm_w': a_norm_w, 'a_in_proj': a_in_proj, 'a_conv_w': a_conv_w,
            'a_conv_b': a_conv_b, 'a_dt_bias': a_dt_bias, 'a_A_log': a_A_log, 'a_D': a_D,
            'a_gnorm_w': a_gnorm_w, 'a_out_proj': a_out_proj, 'kv_norm_w': kv_norm_w,
            'w_kvf': w_kvf, 'b_f': b_f, 'k_norm_w': k_norm_w, 'b_norm_w': b_norm_w,
            'w_q': w_q, 'q_norm_w': q_norm_w, 'w_o': w_o, 'ffn_norm_w': ffn_norm_w,
            'w_gate_up': w_gate_up, 'w_down': w_down}


def reference(x, a_norm_w, a_in_proj, a_conv_w, a_conv_b, a_dt_bias, a_A_log, a_D,
              a_gnorm_w, a_out_proj, kv_norm_w, w_kvf, b_f, k_norm_w, b_norm_w,
              w_q, q_norm_w, w_o, ffn_norm_w, w_gate_up, w_down):
    h = x
    k_sh = v_sh = cum_sh = None
    for layer in range(DEPTH):
        if layer < N_A_LAYERS:
            i = layer
            h = h + mamba2_mixer(rmsnorm(h, a_norm_w[i]), a_in_proj[i], a_conv_w[i], a_conv_b[i],
                                 a_dt_bias[i], a_A_log[i], a_D[i], a_gnorm_w[i], a_out_proj[i])
        else:
            j = layer - N_A_LAYERS
            if j == 0:
                k_sh, v_sh, cum_sh = shared_kv(h, kv_norm_w, w_kvf, b_f, k_norm_w)
            h = h + forgetting_attention(rmsnorm(h, b_norm_w[j]), k_sh, v_sh, cum_sh,
                                         w_q[j], q_norm_w[j], w_o[j])
        h = h + swiglu(rmsnorm(h, ffn_norm_w[layer]), w_gate_up[layer], w_down[layer])
    return h
```

```python
import functools

import jax
import jax.numpy as jnp
from jax import lax
from jax.experimental import pallas as pl
from jax.experimental.pallas import tpu as pltpu

F32 = jnp.float32
BF16 = jnp.bfloat16
EPS = 1e-6

SSM_HEAD_DIM = 64
SSM_GROUPS = 8
D_STATE = 128
CONV_WIDTH = 4
SSD_CHUNK = 128
ATT_HEAD_DIM = 128

LANES = 128
SUBLANES = 8
VMEM_LIMIT_BYTES = 56 * 1024 * 1024

NT_DIMS = (((1,), (1,)), ((), ()))
TN_DIMS = (((0,), (0,)), ((), ()))


def _params(*semantics):
    return pltpu.CompilerParams(dimension_semantics=semantics,
                                vmem_limit_bytes=VMEM_LIMIT_BYTES)


def _sigmoid(v):
    return 0.5 * jnp.tanh(0.5 * v) + 0.5


def _softplus(v):
    return jnp.maximum(v, 0.0) + jnp.log1p(jnp.exp(-jnp.abs(v)))


def _rms_rows(v, w):
    ms = jnp.mean(v * v, axis=-1, keepdims=True)
    return v * lax.rsqrt(ms + EPS) * w


def _tile(n, pref):
    t = min(n, pref)
    assert n % t == 0, (n, pref)
    return t


def _inproj_kernel(x_ref, nw_ref, w_ref, wdt_ref, o_ref, dt_ref, xn_ref):
    @pl.when(pl.program_id(1) == 0)
    def _():
        xn = _rms_rows(x_ref[...], nw_ref[...]).astype(BF16)
        xn_ref[...] = xn
        dt_ref[...] = jnp.dot(xn, wdt_ref[...], preferred_element_type=F32)

    o_ref[...] = jnp.dot(xn_ref[...], w_ref[...],
                         preferred_element_type=F32).astype(o_ref.dtype)


def _in_proj(h, norm_w, w_main, w_dt, tm=1024, tn=1024):
    t, d = h.shape
    n = w_main.shape[1]
    tm, tn = _tile(t, tm), _tile(n, tn)
    return pl.pallas_call(
        _inproj_kernel,
        grid=(t // tm, n // tn),
        in_specs=[
            pl.BlockSpec((tm, d), lambda i, j: (i, 0)),
            pl.BlockSpec((1, d), lambda i, j: (0, 0)),
            pl.BlockSpec((d, tn), lambda i, j: (0, j)),
            pl.BlockSpec((d, LANES), lambda i, j: (0, 0)),
        ],
        out_specs=[
            pl.BlockSpec((tm, tn), lambda i, j: (i, j)),
            pl.BlockSpec((tm, LANES), lambda i, j: (i, 0)),
        ],
        out_shape=[jax.ShapeDtypeStruct((t, n), BF16),
                   jax.ShapeDtypeStruct((t, LANES), F32)],
        scratch_shapes=[pltpu.VMEM((tm, d), BF16)],
        compiler_params=_params("parallel", "arbitrary"),
        name="in_proj",
    )(h, norm_w.reshape(1, d), w_main, w_dt)


def _ssd_kernel(z_ref, x_ref, bc_ref, dt_ref, cwx_ref, cwbc_ref, cbx_ref, cbbc_ref,
                dtb_ref, alog_ref, dskip_ref, gnw_ref, expand_ref, tri_ref,
                y_ref,
                state_ref, xext_ref, bcext_ref, xs_ref, bcs_ref, xdt_ref, xw_ref,
                yacc_ref):
    L = SSD_CHUNK
    P = SSM_HEAD_DIM
    N = D_STATE
    G = SSM_GROUPS
    d_inner = x_ref.shape[1]
    heads_per_group = d_inner // (G * P)
    gw = heads_per_group * P
    halo = SUBLANES

    @pl.when(pl.program_id(1) == 0)
    def _():
        state_ref[...] = jnp.zeros_like(state_ref)
        xext_ref[0:halo, :] = jnp.zeros((halo, xext_ref.shape[1]), F32)
        bcext_ref[0:halo, :] = jnp.zeros((halo, bcext_ref.shape[1]), F32)

    xext_ref[halo:halo + L, :] = x_ref[...].astype(F32)
    bcext_ref[halo:halo + L, :] = bc_ref[...].astype(F32)

    def conv_silu(ext_ref, w_ref, b_ref):
        acc = b_ref[...]
        for k in range(CONV_WIDTH):
            off = halo - (CONV_WIDTH - 1) + k
            acc = acc + w_ref[k:k + 1, :] * ext_ref[off:off + L, :]
        return acc * _sigmoid(acc)

    xs = conv_silu(xext_ref, cwx_ref, cbx_ref)
    bcs_ref[...] = conv_silu(bcext_ref, cwbc_ref, cbbc_ref).astype(BF16)
    xs_ref[...] = xs
    xext_ref[0:halo, :] = xext_ref[L:L + halo, :]
    bcext_ref[0:halo, :] = bcext_ref[L:L + halo, :]

    dtv = _softplus(dt_ref[...] + dtb_ref[...])
    a = dtv * (-jnp.exp(alog_ref[...]))
    acum = jnp.dot(tri_ref[...], a, precision=lax.Precision.HIGHEST,
                   preferred_element_type=F32)
    a_last = acum[L - 1:L, :]
    e_out = jnp.exp(acum)
    e_in = dtv * jnp.exp(a_last - acum)
    acum_t = acum.T

    def expand(v):
        hi = v.astype(BF16)
        lo = (v - hi.astype(F32)).astype(BF16)
        return jnp.dot(jnp.concatenate([hi, lo], axis=1), expand_ref[...],
                       preferred_element_type=F32)

    xdt_ref[...] = (xs * expand(dtv)).astype(BF16)
    xw_ref[...] = (xs * expand(e_in)).astype(BF16)
    e_out_x = expand(e_out)
    state_decay = e_out_x[L - 1:L, :]

    row_id = lax.broadcasted_iota(jnp.int32, (L, L), 0)
    col_id = lax.broadcasted_iota(jnp.int32, (L, L), 1)
    causal = col_id <= row_id
    lane_id = lax.broadcasted_iota(jnp.int32, (L, 2 * P), 1)
    first_head = lane_id < P

    for g in range(G):
        gs = slice(g * gw, (g + 1) * gw)
        b_g = bcs_ref[:, g * N:(g + 1) * N]
        c_g = bcs_ref[:, G * N + g * N:G * N + (g + 1) * N]
        cb = lax.dot_general(c_g, b_g, NT_DIMS, preferred_element_type=F32)
        s_g = state_ref[:, gs]
        y_off = jnp.dot(c_g, s_g.astype(BF16),
                        preferred_element_type=F32) * e_out_x[:, gs]
        upd = lax.dot_general(b_g, xw_ref[:, gs], TN_DIMS,
                              preferred_element_type=F32)
        state_ref[:, gs] = s_g * state_decay[:, gs] + upd
        for pr in range(heads_per_group // 2):
            h0 = g * heads_per_group + 2 * pr
            ms = []
            for hh in (h0, h0 + 1):
                seg = acum[:, hh:hh + 1] - acum_t[hh:hh + 1, :]
                decay = jnp.exp(jnp.where(causal, seg, -jnp.inf))
                ms.append((cb * decay).astype(BF16))
            m_cat = jnp.concatenate(ms, axis=1)
            cs = slice(h0 * P, (h0 + 2) * P)
            xp = xdt_ref[:, cs]
            zero = jnp.zeros_like(xp)
            rhs = jnp.concatenate([jnp.where(first_head, xp, zero),
                                   jnp.where(first_head, zero, xp)], axis=0)
            y_diag = jnp.dot(m_cat, rhs, preferred_element_type=F32)
            yacc_ref[:, cs] = y_diag + y_off[:, pr * 2 * P:(pr + 1) * 2 * P]

    zf = z_ref[...].astype(F32)
    y = (yacc_ref[...] + xs_ref[...] * dskip_ref[...]) * (zf * _sigmoid(zf))
    for g in range(G):
        gs = slice(g * gw, (g + 1) * gw)
        y_ref[:, gs] = _rms_rows(y[:, gs], gnw_ref[:, gs]).astype(y_ref.dtype)


def _ssd(zx, dt_raw, conv_w, conv_b, dt_bias, a_log, d_skip, gnorm_w, bsz, seq):
    t = zx.shape[0]
    L, P, N, G = SSD_CHUNK, SSM_HEAD_DIM, D_STATE, SSM_GROUPS
    n_heads = dt_bias.shape[0]
    d_inner = n_heads * P
    d_bc = 2 * G * N
    assert zx.shape[1] == 2 * d_inner + d_bc and n_heads <= LANES
    assert d_inner % d_bc == 0 and seq % L == 0
    nc = seq // L
    bc_blk = 2 * d_inner // d_bc

    pad_h = LANES - n_heads
    dtb = jnp.pad(dt_bias.astype(F32), (0, pad_h)).reshape(1, LANES)
    alog = jnp.pad(a_log.astype(F32), (0, pad_h)).reshape(1, LANES)
    dskip_x = jnp.repeat(d_skip.astype(F32), P).reshape(1, d_inner)
    head_of_channel = jnp.arange(d_inner, dtype=jnp.int32) // P
    sel = (jnp.arange(LANES, dtype=jnp.int32)[:, None] == head_of_channel[None, :])
    expand_mat = jnp.concatenate([sel, sel], axis=0).astype(BF16)
    tri = (jnp.arange(L)[None, :] <= jnp.arange(L)[:, None]).astype(F32)

    row = lambda b, c: b * nc + c
    const = lambda b, c: (0, 0)
    return pl.pallas_call(
        _ssd_kernel,
        grid=(bsz, nc),
        in_specs=[
            pl.BlockSpec((L, d_inner), lambda b, c: (row(b, c), 0)),
            pl.BlockSpec((L, d_inner), lambda b, c: (row(b, c), 1)),
            pl.BlockSpec((L, d_bc), lambda b, c: (row(b, c), bc_blk)),
            pl.BlockSpec((L, LANES), lambda b, c: (row(b, c), 0)),
            pl.BlockSpec((CONV_WIDTH, d_inner), const),
            pl.BlockSpec((CONV_WIDTH, d_bc), const),
            pl.BlockSpec((1, d_inner), const),
            pl.BlockSpec((1, d_bc), const),
            pl.BlockSpec((1, LANES), const),
            pl.BlockSpec((1, LANES), const),
            pl.BlockSpec((1, d_inner), const),
            pl.BlockSpec((1, d_inner), const),
            pl.BlockSpec((2 * LANES, d_inner), const),
            pl.BlockSpec((L, L), const),
        ],
        out_specs=pl.BlockSpec((L, d_inner), lambda b, c: (row(b, c), 0)),
        out_shape=jax.ShapeDtypeStruct((t, d_inner), BF16),
        scratch_shapes=[
            pltpu.VMEM((N, d_inner), F32),
            pltpu.VMEM((L + 2 * SUBLANES, d_inner), F32),
            pltpu.VMEM((L + 2 * SUBLANES, d_bc), F32),
            pltpu.VMEM((L, d_inner), F32),
            pltpu.VMEM((L, d_bc), BF16),
            pltpu.VMEM((L, d_inner), BF16),
            pltpu.VMEM((L, d_inner), BF16),
            pltpu.VMEM((L, d_inner), F32),
        ],
        compiler_params=_params("parallel", "arbitrary"),
        name="ssd",
    )(zx, zx, zx, dt_raw,
      conv_w[:, :d_inner], conv_w[:, d_inner:],
      conv_b[:d_inner].reshape(1, d_inner), conv_b[d_inner:].reshape(1, d_bc),
      dtb, alog, dskip_x, gnorm_w.reshape(1, d_inner), expand_mat, tri)


def _mm_res_kernel(a_ref, w_ref, r_ref, o_ref):
    o_ref[...] = r_ref[...] + jnp.dot(a_ref[...], w_ref[...], preferred_element_type=F32)


def _mm_res(a, w, res, tm=1024, tn=512, name="mm_res"):
    t, k = a.shape
    n = w.shape[1]
    tm, tn = _tile(t, tm), _tile(n, tn)
    return pl.pallas_call(
        _mm_res_kernel,
        grid=(t // tm, n // tn),
        in_specs=[
            pl.BlockSpec((tm, k), lambda i, j: (i, 0)),
            pl.BlockSpec((k, tn), lambda i, j: (0, j)),
            pl.BlockSpec((tm, tn), lambda i, j: (i, j)),
        ],
        out_specs=pl.BlockSpec((tm, tn), lambda i, j: (i, j)),
        out_shape=jax.ShapeDtypeStruct((t, n), F32),
        compiler_params=_params("parallel", "arbitrary"),
        name=name,
    )(a, w, res)


def _ffn_up_kernel(x_ref, nw_ref, wg_ref, wu_ref, o_ref, xn_ref):
    @pl.when(pl.program_id(1) == 0)
    def _():
        xn_ref[...] = _rms_rows(x_ref[...], nw_ref[...]).astype(BF16)

    xn = xn_ref[...]
    g = jnp.dot(xn, wg_ref[...], preferred_element_type=F32)
    u = jnp.dot(xn, wu_ref[...], preferred_element_type=F32)
    o_ref[...] = (g * _sigmoid(g) * u).astype(o_ref.dtype)


def _ffn_up(h, norm_w, w_gate_up, tm=1024, tn=512):
    t, d = h.shape
    d_ff = w_gate_up.shape[1] // 2
    tm, tn = _tile(t, tm), _tile(d_ff, tn)
    nj = d_ff // tn
    return pl.pallas_call(
        _ffn_up_kernel,
        grid=(t // tm, nj),
        in_specs=[
            pl.BlockSpec((tm, d), lambda i, j: (i, 0)),
            pl.BlockSpec((1, d), lambda i, j: (0, 0)),
            pl.BlockSpec((d, tn), lambda i, j: (0, j)),
            pl.BlockSpec((d, tn), lambda i, j: (0, j + nj)),
        ],
        out_specs=pl.BlockSpec((tm, tn), lambda i, j: (i, j)),
        out_shape=jax.ShapeDtypeStruct((t, d_ff), BF16),
        scratch_shapes=[pltpu.VMEM((tm, d), BF16)],
        compiler_params=_params("parallel", "arbitrary"),
        name="ffn_up",
    )(h, norm_w.reshape(1, d), w_gate_up, w_gate_up)


def _ffn(h, norm_w, w_gate_up, w_down):
    act = _ffn_up(h, norm_w, w_gate_up.astype(BF16))
    return _mm_res(act, w_down.astype(BF16), h, name="ffn_down")


def _head_norm(acc, hw, scale):
    parts = []
    for c in range(acc.shape[1] // ATT_HEAD_DIM):
        part = acc[:, c * ATT_HEAD_DIM:(c + 1) * ATT_HEAD_DIM]
        parts.append(_rms_rows(part, hw) * scale)
    return jnp.concatenate(parts, axis=1)


def _qproj_kernel(x_ref, nw_ref, w_ref, hw_ref, o_ref, xn_ref, *, scale):
    @pl.when(pl.program_id(1) == 0)
    def _():
        xn_ref[...] = _rms_rows(x_ref[...], nw_ref[...]).astype(BF16)

    acc = jnp.dot(xn_ref[...], w_ref[...], preferred_element_type=F32)
    o_ref[...] = _head_norm(acc, hw_ref[...], scale).astype(o_ref.dtype)


def _q_proj(h, norm_w, w, head_w, scale, tm=1024, tn=512):
    t, d = h.shape
    n = w.shape[1]
    tm, tn = _tile(t, tm), _tile(n, tn)
    return pl.pallas_call(
        functools.partial(_qproj_kernel, scale=scale),
        grid=(t // tm, n // tn),
        in_specs=[
            pl.BlockSpec((tm, d), lambda i, j: (i, 0)),
            pl.BlockSpec((1, d), lambda i, j: (0, 0)),
            pl.BlockSpec((d, tn), lambda i, j: (0, j)),
            pl.BlockSpec((1, ATT_HEAD_DIM), lambda i, j: (0, 0)),
        ],
        out_specs=pl.BlockSpec((tm, tn), lambda i, j: (i, j)),
        out_shape=jax.ShapeDtypeStruct((t, n), BF16),
        scratch_shapes=[pltpu.VMEM((tm, d), BF16)],
        compiler_params=_params("parallel", "arbitrary"),
        name="q_proj",
    )(h, norm_w.reshape(1, d), w, head_w.reshape(1, ATT_HEAD_DIM))


def _kvproj_kernel(x_ref, nw_ref, w_ref, wf_ref, bf_ref, hw_ref, tri_ref,
                   o_ref, cum_ref, xn_ref, carry_ref, *, n_k_blocks, blocks_per_seq):
    i = pl.program_id(0)
    j = pl.program_id(1)
    tm = x_ref.shape[0]

    @pl.when(j == 0)
    def _():
        xn = _rms_rows(x_ref[...], nw_ref[...]).astype(BF16)
        xn_ref[...] = xn
        logit = jnp.dot(xn, wf_ref[...], preferred_element_type=F32) + bf_ref[...]
        log_f = -_softplus(-logit)

        @pl.when(i % blocks_per_seq == 0)
        def _():
            carry_ref[...] = jnp.zeros_like(carry_ref)

        cum = jnp.dot(tri_ref[...], log_f, precision=lax.Precision.HIGHEST,
                      preferred_element_type=F32) + carry_ref[0:1, :]
        cum_ref[...] = cum
        carry_ref[...] = jnp.broadcast_to(cum[tm - 1:tm, :], carry_ref.shape)

    acc = jnp.dot(xn_ref[...], w_ref[...], preferred_element_type=F32)

    @pl.when(j < n_k_blocks)
    def _():
        o_ref[...] = _head_norm(acc, hw_ref[...], 1.0).astype(o_ref.dtype)

    @pl.when(j >= n_k_blocks)
    def _():
        o_ref[...] = acc.astype(o_ref.dtype)


def _kv_proj(h, norm_w, w_kv, w_f, b_f, head_w, d_att, seq, tm=512, tn=512):
    t, d = h.shape
    n = w_kv.shape[1]
    tm, tn = _tile(seq, tm), _tile(d_att, tn)
    tri = (jnp.arange(tm)[None, :] <= jnp.arange(tm)[:, None]).astype(F32)
    kern = functools.partial(_kvproj_kernel, n_k_blocks=d_att // tn,
                             blocks_per_seq=seq // tm)
    return pl.pallas_call(
        kern,
        grid=(t // tm, n // tn),
        in_specs=[
            pl.BlockSpec((tm, d), lambda i, j: (i, 0)),
            pl.BlockSpec((1, d), lambda i, j: (0, 0)),
            pl.BlockSpec((d, tn), lambda i, j: (0, j)),
            pl.BlockSpec((d, LANES), lambda i, j: (0, 0)),
            pl.BlockSpec((1, LANES), lambda i, j: (0, 0)),
            pl.BlockSpec((1, ATT_HEAD_DIM), lambda i, j: (0, 0)),
            pl.BlockSpec((tm, tm), lambda i, j: (0, 0)),
        ],
        out_specs=[
            pl.BlockSpec((tm, tn), lambda i, j: (i, j)),
            pl.BlockSpec((tm, LANES), lambda i, j: (i, 0)),
        ],
        out_shape=[jax.ShapeDtypeStruct((t, n), BF16),
                   jax.ShapeDtypeStruct((t, LANES), F32)],
        scratch_shapes=[pltpu.VMEM((tm, d), BF16), pltpu.VMEM((SUBLANES, LANES), F32)],
        compiler_params=_params("arbitrary", "arbitrary"),
        name="kv_proj",
    )(h, norm_w.reshape(1, d), w_kv, w_f, b_f, head_w.reshape(1, ATT_HEAD_DIM), tri)


def _attn_kernel(q_ref, k_ref, v_ref, nc_ref, o_ref, *, tq):
    seq = q_ref.shape[0]
    nq = seq // tq
    row_id = lax.broadcasted_iota(jnp.int32, (tq, tq), 0)
    col_id = lax.broadcasted_iota(jnp.int32, (tq, tq), 1)
    causal = col_id <= row_id

    def block(q, start, carry, mask):
        m, l, acc = carry
        k = k_ref[pl.ds(start, tq), :]
        v = v_ref[pl.ds(start, tq), :]
        s = lax.dot_general(q, k, NT_DIMS, preferred_element_type=F32)
        s = s + nc_ref[:, pl.ds(start, tq)]
        if mask:
            s = jnp.where(causal, s, -jnp.inf)
        m_new = jnp.maximum(m, jnp.max(s, axis=-1, keepdims=True))
        alpha = jnp.exp(m - m_new)
        p = jnp.exp(s - m_new)
        l = alpha * l + jnp.sum(p, axis=-1, keepdims=True)
        acc = alpha * acc + jnp.dot(p.astype(BF16), v, preferred_element_type=F32)
        return m_new, l, acc

    for qi in range(nq):
        q = q_ref[qi * tq:(qi + 1) * tq, :]
        carry = (jnp.full((tq, 1), -jnp.inf, F32), jnp.zeros((tq, 1), F32),
                 jnp.zeros((tq, ATT_HEAD_DIM), F32))
        if qi > 0:
            carry = lax.fori_loop(
                0, qi,
                lambda ki, c: block(q, pl.multiple_of(ki * tq, tq), c, False),
                carry)
        _, l, acc = block(q, qi * tq, carry, True)
        o_ref[qi * tq:(qi + 1) * tq, :] = (acc / l).astype(o_ref.dtype)


def _attention(q, kv, neg_cum_t, bsz, seq, n_heads, tq=512):
    t = q.shape[0]
    tq = _tile(seq, tq)
    hd = ATT_HEAD_DIM
    return pl.pallas_call(
        functools.partial(_attn_kernel, tq=tq),
        grid=(bsz, n_heads),
        in_specs=[
            pl.BlockSpec((seq, hd), lambda b, h: (b, h)),
            pl.BlockSpec((seq, hd), lambda b, h: (b, h)),
            pl.BlockSpec((seq, hd), lambda b, h: (b, n_heads + h)),
            pl.BlockSpec((None, 1, seq), lambda b, h: (b * n_heads + h, 0, 0)),
        ],
        out_specs=pl.BlockSpec((seq, hd), lambda b, h: (b, h)),
        out_shape=jax.ShapeDtypeStruct((t, n_heads * hd), BF16),
        compiler_params=_params("parallel", "arbitrary"),
        name="fox_attention",
    )(q, kv, kv, neg_cum_t)


def kernel(x, a_norm_w, a_in_proj, a_conv_w, a_conv_b, a_dt_bias, a_A_log, a_D, a_gnorm_w, a_out_proj, kv_norm_w, w_kvf, b_f, k_norm_w, b_norm_w, w_q, q_norm_w, w_o, ffn_norm_w, w_gate_up, w_down):
    bsz, seq, d_model = x.shape
    t = bsz * seq
    n_a = a_norm_w.shape[0]
    n_b = b_norm_w.shape[0]
    h = x.reshape(t, d_model).astype(F32)

    for i in range(n_a):
        n_ssm_heads = a_dt_bias.shape[1]
        n_main = a_in_proj.shape[2] - n_ssm_heads
        w_main = a_in_proj[i, :, :n_main].astype(BF16)
        w_dt = jnp.pad(a_in_proj[i, :, n_main:], ((0, 0), (0, LANES - n_ssm_heads))).astype(BF16)
        zx, dt_raw = _in_proj(h, a_norm_w[i], w_main, w_dt)
        y = _ssd(zx, dt_raw, a_conv_w[i], a_conv_b[i], a_dt_bias[i], a_A_log[i], a_D[i],
                 a_gnorm_w[i], bsz, seq)
        h = _mm_res(y, a_out_proj[i].astype(BF16), h, name="out_proj")
        h = _ffn(h, ffn_norm_w[i], w_gate_up[i], w_down[i])

    n_att_heads = b_f.shape[0]
    d_att = n_att_heads * ATT_HEAD_DIM
    scale = ATT_HEAD_DIM ** -0.5
    kv = neg_cum_t = None
    for j in range(n_b):
        if j == 0:
            w_kv = w_kvf[:, :2 * d_att].astype(BF16)
            w_f = jnp.pad(w_kvf[:, 2 * d_att:], ((0, 0), (0, LANES - n_att_heads))).astype(BF16)
            b_pad = jnp.pad(b_f.astype(F32), (0, LANES - n_att_heads)).reshape(1, LANES)
            kv, cum = _kv_proj(h, kv_norm_w, w_kv, w_f, b_pad, k_norm_w, d_att, seq)
            neg_cum_t = (-cum[:, :n_att_heads]).reshape(bsz, seq, n_att_heads)
            neg_cum_t = neg_cum_t.transpose(0, 2, 1).reshape(bsz * n_att_heads, 1, seq)
        q = _q_proj(h, b_norm_w[j], w_q[j].astype(BF16), q_norm_w[j], scale)
        o = _attention(q, kv, neg_cum_t, bsz, seq, n_att_heads)
        h = _mm_res(o, w_o[j].astype(BF16), h, name="attn_out")
        h = _ffn(h, ffn_norm_w[n_a + j], w_gate_up[n_a + j], w_down[n_a + j])

    return h.reshape(bsz, seq, d_model).astype(x.dtype)
```

```python
import functools

import jax
import jax.numpy as jnp
from jax import lax
from jax.experimental import pallas as pl
from jax.experimental.pallas import tpu as pltpu

F32 = jnp.float32
BF16 = jnp.bfloat16
EPS = 1e-6

SSM_HEAD_DIM = 64
SSM_GROUPS = 8
D_STATE = 128
CONV_WIDTH = 4
SSD_CHUNK = 128
ATT_HEAD_DIM = 128

LANES = 128
SUBLANES = 8
BF16_SUBLANES = 16
VMEM_LIMIT_BYTES = 56 * 1024 * 1024

CONV_HALO = BF16_SUBLANES
LOG2E = 1.4426950408889634

NT_DIMS = (((1,), (1,)), ((), ()))
TN_DIMS = (((0,), (0,)), ((), ()))


def _params(*semantics):
    return pltpu.CompilerParams(dimension_semantics=semantics,
                                vmem_limit_bytes=VMEM_LIMIT_BYTES)


def _silu(v):
    half = 0.5 * v
    return half + half * jnp.tanh(half)


def _softplus(v):
    return jnp.maximum(v, 0.0) + jnp.log1p(jnp.exp(-jnp.abs(v)))


def _rms_rows(v, w):
    ms = jnp.mean(v * v, axis=-1, keepdims=True)
    return v * lax.rsqrt(ms + EPS) * w


SPLIT_PIECES = 3


def _bf16_pieces(v, n):
    parts, rest = [], v
    for _ in range(n):
        part = rest.astype(BF16)
        parts.append(part)
        rest = rest - part.astype(F32)
    return parts


def _scan_matrix(rows):
    tri = jnp.arange(rows)[None, :] <= jnp.arange(rows)[:, None]
    return jnp.concatenate([tri] * SPLIT_PIECES, axis=1).astype(BF16)


def _cumsum_rows(v, scan_ref):
    stacked = jnp.concatenate(_bf16_pieces(v, SPLIT_PIECES), axis=0)
    return jnp.dot(scan_ref[...], stacked, preferred_element_type=F32)


def _tile(n, pref):
    t = min(n, pref)
    assert n % t == 0, (n, pref)
    return t


def _inproj_kernel(x_ref, nw_ref, w_ref, wdt_ref, o_ref, dt_ref, xn_ref):
    @pl.when(pl.program_id(1) == 0)
    def _():
        xn = _rms_rows(x_ref[...], nw_ref[...]).astype(BF16)
        xn_ref[...] = xn
        dt_ref[...] = jnp.dot(xn, wdt_ref[...], preferred_element_type=F32)

    o_ref[...] = jnp.dot(xn_ref[...], w_ref[...],
                         preferred_element_type=F32).astype(o_ref.dtype)


def _in_proj(h, norm_w, w_main, w_dt, tm=1024, tn=1024):
    t, d = h.shape
    n = w_main.shape[1]
    tm, tn = _tile(t, tm), _tile(n, tn)
    return pl.pallas_call(
        _inproj_kernel,
        grid=(t // tm, n // tn),
        in_specs=[
            pl.BlockSpec((tm, d), lambda i, j: (i, 0)),
            pl.BlockSpec((1, d), lambda i, j: (0, 0)),
            pl.BlockSpec((d, tn), lambda i, j: (0, j)),
            pl.BlockSpec((d, LANES), lambda i, j: (0, 0)),
        ],
        out_specs=[
            pl.BlockSpec((tm, tn), lambda i, j: (i, j)),
            pl.BlockSpec((tm, LANES), lambda i, j: (i, 0)),
        ],
        out_shape=[jax.ShapeDtypeStruct((t, n), BF16),
                   jax.ShapeDtypeStruct((t, LANES), F32)],
        scratch_shapes=[pltpu.VMEM((tm, d), BF16)],
        compiler_params=_params("parallel", "arbitrary"),
        name="in_proj",
    )(h, norm_w.reshape(1, d), w_main, w_dt)


def _ssd_kernel(z_ref, x_ref, bc_ref, dt_ref, cwx_ref, cwbc_ref, cbx_ref, cbbc_ref,
                dtb_ref, alog_ref, dskip_ref, gnw_ref, expand_ref, tri_ref, shift_ref,
                y_ref,
                state_ref, xext_ref, bcext_ref, bcs_ref):
    L = SSD_CHUNK
    P = SSM_HEAD_DIM
    N = D_STATE
    G = SSM_GROUPS
    d_inner = x_ref.shape[1]
    d_bc = bc_ref.shape[1]
    heads_per_group = d_inner // (G * P)
    gw = heads_per_group * P
    halo = CONV_HALO

    @pl.when(pl.program_id(1) == 0)
    def _():
        state_ref[...] = jnp.zeros_like(state_ref)
        xext_ref[0:halo, :] = jnp.zeros((halo, d_inner), BF16)
        bcext_ref[0:halo, :] = jnp.zeros((halo, d_bc), BF16)

    xext_ref[halo:halo + L, :] = x_ref[...]
    bcext_ref[halo:halo + L, :] = bc_ref[...]

    def conv_silu(ext_ref, cur_ref, w_ref, b_ref, cols):
        shifted = jnp.dot(shift_ref[...], ext_ref[:, cols], preferred_element_type=F32)
        acc = b_ref[:, cols] + w_ref[CONV_WIDTH - 1:CONV_WIDTH, cols] * cur_ref[:, cols].astype(F32)
        for k in range(CONV_WIDTH - 1):
            acc = acc + w_ref[k:k + 1, cols] * shifted[k * L:(k + 1) * L, :]
        return _silu(acc)

    for j in range(d_bc // gw):
        cols = slice(j * gw, (j + 1) * gw)
        bcs_ref[:, cols] = conv_silu(bcext_ref, bc_ref, cwbc_ref, cbbc_ref, cols).astype(BF16)
    bcext_ref[0:halo, :] = bcext_ref[L:L + halo, :]

    dtv = _softplus(dt_ref[...] + dtb_ref[...])
    a = dtv * (-LOG2E * jnp.exp(alog_ref[...]))
    acum = _cumsum_rows(a, tri_ref)
    a_last = acum[L - 1:L, :]
    acum_t = acum.T

    def split(v):
        return jnp.concatenate(_bf16_pieces(v, 2), axis=1)

    dt_hl = split(dtv)
    e_out_hl = split(jnp.exp2(acum))
    e_in_hl = split(dtv * jnp.exp2(a_last - acum))

    def expand(hl, cols):
        return jnp.dot(hl, expand_ref[:, cols], preferred_element_type=F32)

    row_id = lax.broadcasted_iota(jnp.int32, (L, L), 0)
    col_id = lax.broadcasted_iota(jnp.int32, (L, L), 1)
    causal = col_id <= row_id
    lane_id = lax.broadcasted_iota(jnp.int32, (L, 2 * P), 1)
    first_head = lane_id < P

    for g in range(G):
        gs = slice(g * gw, (g + 1) * gw)
        xs = conv_silu(xext_ref, x_ref, cwx_ref, cbx_ref, gs)
        xdt = (xs * expand(dt_hl, gs)).astype(BF16)
        xw = (xs * expand(e_in_hl, gs)).astype(BF16)
        e_out_x = expand(e_out_hl, gs)
        b_g = bcs_ref[:, g * N:(g + 1) * N]
        c_g = bcs_ref[:, G * N + g * N:G * N + (g + 1) * N]
        cb = lax.dot_general(c_g, b_g, NT_DIMS, preferred_element_type=F32)
        s_g = state_ref[:, gs]
        y_off = jnp.dot(c_g, s_g.astype(BF16), preferred_element_type=F32) * e_out_x
        upd = lax.dot_general(b_g, xw, TN_DIMS, preferred_element_type=F32)
        state_ref[:, gs] = s_g * e_out_x[L - 1:L, :] + upd
        y_pairs = []
        for pr in range(heads_per_group // 2):
            h0 = g * heads_per_group + 2 * pr
            ms = []
            for hh in (h0, h0 + 1):
                seg = acum[:, hh:hh + 1] - acum_t[hh:hh + 1, :]
                decay = jnp.exp2(jnp.where(causal, seg, -jnp.inf))
                ms.append((cb * decay).astype(BF16))
            m_cat = jnp.concatenate(ms, axis=1)
            xp = xdt[:, pr * 2 * P:(pr + 1) * 2 * P]
            zero = jnp.zeros_like(xp)
            rhs = jnp.concatenate([jnp.where(first_head, xp, zero),
                                   jnp.where(first_head, zero, xp)], axis=0)
            y_pairs.append(jnp.dot(m_cat, rhs, preferred_element_type=F32))
        y = jnp.concatenate(y_pairs, axis=1) + y_off + xs * dskip_ref[:, gs]
        y = y * _silu(z_ref[:, gs].astype(F32))
        y_ref[:, gs] = _rms_rows(y, gnw_ref[:, gs]).astype(y_ref.dtype)
    xext_ref[0:halo, :] = xext_ref[L:L + halo, :]


def _ssd(zx, dt_raw, conv_w, conv_b, dt_bias, a_log, d_skip, gnorm_w, bsz, seq):
    t = zx.shape[0]
    L, P, N, G = SSD_CHUNK, SSM_HEAD_DIM, D_STATE, SSM_GROUPS
    n_heads = dt_bias.shape[0]
    d_inner = n_heads * P
    d_bc = 2 * G * N
    assert zx.shape[1] == 2 * d_inner + d_bc and n_heads <= LANES
    assert d_inner % d_bc == 0 and seq % L == 0
    nc = seq // L
    bc_blk = 2 * d_inner // d_bc

    pad_h = LANES - n_heads
    dtb = jnp.pad(dt_bias.astype(F32), (0, pad_h)).reshape(1, LANES)
    alog = jnp.pad(a_log.astype(F32), (0, pad_h)).reshape(1, LANES)
    dskip_x = jnp.repeat(d_skip.astype(F32), P).reshape(1, d_inner)
    head_of_channel = jnp.arange(d_inner, dtype=jnp.int32) // P
    sel = (jnp.arange(LANES, dtype=jnp.int32)[:, None] == head_of_channel[None, :])
    expand_mat = jnp.concatenate([sel, sel], axis=0).astype(BF16)
    tri = _scan_matrix(L)
    n_shift = CONV_WIDTH - 1
    src_row = (CONV_HALO - n_shift + jnp.arange(n_shift)[:, None] + jnp.arange(L)[None, :]).reshape(-1)
    shift_mat = (src_row[:, None] == jnp.arange(CONV_HALO + L)[None, :]).astype(BF16)

    row = lambda b, c: b * nc + c
    const = lambda b, c: (0, 0)
    return pl.pallas_call(
        _ssd_kernel,
        grid=(bsz, nc),
        in_specs=[
            pl.BlockSpec((L, d_inner), lambda b, c: (row(b, c), 0)),
            pl.BlockSpec((L, d_inner), lambda b, c: (row(b, c), 1)),
            pl.BlockSpec((L, d_bc), lambda b, c: (row(b, c), bc_blk)),
            pl.BlockSpec((L, LANES), lambda b, c: (row(b, c), 0)),
            pl.BlockSpec((CONV_WIDTH, d_inner), const),
            pl.BlockSpec((CONV_WIDTH, d_bc), const),
            pl.BlockSpec((1, d_inner), const),
            pl.BlockSpec((1, d_bc), const),
            pl.BlockSpec((1, LANES), const),
            pl.BlockSpec((1, LANES), const),
            pl.BlockSpec((1, d_inner), const),
            pl.BlockSpec((1, d_inner), const),
            pl.BlockSpec((2 * LANES, d_inner), const),
            pl.BlockSpec(tri.shape, const),
            pl.BlockSpec((n_shift * L, CONV_HALO + L), const),
        ],
        out_specs=pl.BlockSpec((L, d_inner), lambda b, c: (row(b, c), 0)),
        out_shape=jax.ShapeDtypeStruct((t, d_inner), BF16),
        scratch_shapes=[
            pltpu.VMEM((N, d_inner), F32),
            pltpu.VMEM((CONV_HALO + L, d_inner), BF16),
            pltpu.VMEM((CONV_HALO + L, d_bc), BF16),
            pltpu.VMEM((L, d_bc), BF16),
        ],
        compiler_params=_params("parallel", "arbitrary"),
        name="ssd",
    )(zx, zx, zx, dt_raw,
      conv_w[:, :d_inner], conv_w[:, d_inner:],
      conv_b[:d_inner].reshape(1, d_inner), conv_b[d_inner:].reshape(1, d_bc),
      dtb, alog, dskip_x, gnorm_w.reshape(1, d_inner), expand_mat, tri, shift_mat)


def _mm_res_kernel(a_ref, w_ref, r_ref, o_ref):
    o_ref[...] = r_ref[...] + jnp.dot(a_ref[...], w_ref[...], preferred_element_type=F32)


def _mm_res(a, w, res, tm=1024, tn=512, name="mm_res"):
    t, k = a.shape
    n = w.shape[1]
    tm, tn = _tile(t, tm), _tile(n, tn)
    return pl.pallas_call(
        _mm_res_kernel,
        grid=(t // tm, n // tn),
        in_specs=[
            pl.BlockSpec((tm, k), lambda i, j: (i, 0)),
            pl.BlockSpec((k, tn), lambda i, j: (0, j)),
            pl.BlockSpec((tm, tn), lambda i, j: (i, j)),
        ],
        out_specs=pl.BlockSpec((tm, tn), lambda i, j: (i, j)),
        out_shape=jax.ShapeDtypeStruct((t, n), F32),
        compiler_params=_params("parallel", "arbitrary"),
        name=name,
    )(a, w, res)


def _ffn_up_kernel(x_ref, nw_ref, wg_ref, wu_ref, o_ref, xn_ref):
    @pl.when(pl.program_id(1) == 0)
    def _():
        xn_ref[...] = _rms_rows(x_ref[...], nw_ref[...]).astype(BF16)

    xn = xn_ref[...]
    g = jnp.dot(xn, wg_ref[...], preferred_element_type=F32)
    u = jnp.dot(xn, wu_ref[...], preferred_element_type=F32)
    o_ref[...] = (_silu(g) * u).astype(o_ref.dtype)


def _ffn_up(h, norm_w, w_gate_up, tm=1024, tn=512):
    t, d = h.shape
    d_ff = w_gate_up.shape[1] // 2
    tm, tn = _tile(t, tm), _tile(d_ff, tn)
    nj = d_ff // tn
    return pl.pallas_call(
        _ffn_up_kernel,
        grid=(t // tm, nj),
        in_specs=[
            pl.BlockSpec((tm, d), lambda i, j: (i, 0)),
            pl.BlockSpec((1, d), lambda i, j: (0, 0)),
            pl.BlockSpec((d, tn), lambda i, j: (0, j)),
            pl.BlockSpec((d, tn), lambda i, j: (0, j + nj)),
        ],
        out_specs=pl.BlockSpec((tm, tn), lambda i, j: (i, j)),
        out_shape=jax.ShapeDtypeStruct((t, d_ff), BF16),
        scratch_shapes=[pltpu.VMEM((tm, d), BF16)],
        compiler_params=_params("parallel", "arbitrary"),
        name="ffn_up",
    )(h, norm_w.reshape(1, d), w_gate_up, w_gate_up)


def _ffn(h, norm_w, w_gate_up, w_down):
    act = _ffn_up(h, norm_w, w_gate_up.astype(BF16))
    return _mm_res(act, w_down.astype(BF16), h, name="ffn_down")


def _head_norm(acc, hw, scale):
    parts = []
    for c in range(acc.shape[1] // ATT_HEAD_DIM):
        part = acc[:, c * ATT_HEAD_DIM:(c + 1) * ATT_HEAD_DIM]
        parts.append(_rms_rows(part, hw) * scale)
    return jnp.concatenate(parts, axis=1)


def _qproj_kernel(x_ref, nw_ref, w_ref, hw_ref, o_ref, xn_ref, *, scale):
    @pl.when(pl.program_id(1) == 0)
    def _():
        xn_ref[...] = _rms_rows(x_ref[...], nw_ref[...]).astype(BF16)

    acc = jnp.dot(xn_ref[...], w_ref[...], preferred_element_type=F32)
    o_ref[...] = _head_norm(acc, hw_ref[...], scale).astype(o_ref.dtype)


def _q_proj(h, norm_w, w, head_w, scale, tm=1024, tn=512):
    t, d = h.shape
    n = w.shape[1]
    tm, tn = _tile(t, tm), _tile(n, tn)
    return pl.pallas_call(
        functools.partial(_qproj_kernel, scale=scale),
        grid=(t // tm, n // tn),
        in_specs=[
            pl.BlockSpec((tm, d), lambda i, j: (i, 0)),
            pl.BlockSpec((1, d), lambda i, j: (0, 0)),
            pl.BlockSpec((d, tn), lambda i, j: (0, j)),
            pl.BlockSpec((1, ATT_HEAD_DIM), lambda i, j: (0, 0)),
        ],
        out_specs=pl.BlockSpec((tm, tn), lambda i, j: (i, j)),
        out_shape=jax.ShapeDtypeStruct((t, n), BF16),
        scratch_shapes=[pltpu.VMEM((tm, d), BF16)],
        compiler_params=_params("parallel", "arbitrary"),
        name="q_proj",
    )(h, norm_w.reshape(1, d), w, head_w.reshape(1, ATT_HEAD_DIM))


def _kvproj_kernel(x_ref, nw_ref, w_ref, wf_ref, bf_ref, hw_ref, tri_ref,
                   o_ref, cum_ref, xn_ref, carry_ref, *, n_k_blocks, blocks_per_seq):
    i = pl.program_id(0)
    j = pl.program_id(1)
    tm = x_ref.shape[0]

    @pl.when(j == 0)
    def _():
        xn = _rms_rows(x_ref[...], nw_ref[...]).astype(BF16)
        xn_ref[...] = xn
        logit = jnp.dot(xn, wf_ref[...], preferred_element_type=F32) + bf_ref[...]
        log_f = -_softplus(-logit)

        @pl.when(i % blocks_per_seq == 0)
        def _():
            carry_ref[...] = jnp.zeros_like(carry_ref)

        rows_per_scan = tri_ref.shape[0]
        carry = carry_ref[0:1, :]
        for sb in range(tm // rows_per_scan):
            rows = slice(sb * rows_per_scan, (sb + 1) * rows_per_scan)
            cum = _cumsum_rows(log_f[rows, :], tri_ref) + carry
            cum_ref[rows, :] = cum
            carry = cum[rows_per_scan - 1:rows_per_scan, :]
        carry_ref[...] = jnp.broadcast_to(carry, carry_ref.shape)

    acc = jnp.dot(xn_ref[...], w_ref[...], preferred_element_type=F32)

    @pl.when(j < n_k_blocks)
    def _():
        o_ref[...] = _head_norm(acc, hw_ref[...], 1.0).astype(o_ref.dtype)

    @pl.when(j >= n_k_blocks)
    def _():
        o_ref[...] = acc.astype(o_ref.dtype)


def _kv_proj(h, norm_w, w_kv, w_f, b_f, head_w, d_att, seq, tm=1024, tn=512, scan_rows=256):
    t, d = h.shape
    n = w_kv.shape[1]
    tm, tn = _tile(seq, tm), _tile(d_att, tn)
    tri = _scan_matrix(_tile(tm, scan_rows))
    kern = functools.partial(_kvproj_kernel, n_k_blocks=d_att // tn,
                             blocks_per_seq=seq // tm)
    return pl.pallas_call(
        kern,
        grid=(t // tm, n // tn),
        in_specs=[
            pl.BlockSpec((tm, d), lambda i, j: (i, 0)),
            pl.BlockSpec((1, d), lambda i, j: (0, 0)),
            pl.BlockSpec((d, tn), lambda i, j: (0, j)),
            pl.BlockSpec((d, LANES), lambda i, j: (0, 0)),
            pl.BlockSpec((1, LANES), lambda i, j: (0, 0)),
            pl.BlockSpec((1, ATT_HEAD_DIM), lambda i, j: (0, 0)),
            pl.BlockSpec(tri.shape, lambda i, j: (0, 0)),
        ],
        out_specs=[
            pl.BlockSpec((tm, tn), lambda i, j: (i, j)),
            pl.BlockSpec((tm, LANES), lambda i, j: (i, 0)),
        ],
        out_shape=[jax.ShapeDtypeStruct((t, n), BF16),
                   jax.ShapeDtypeStruct((t, LANES), F32)],
        scratch_shapes=[pltpu.VMEM((tm, d), BF16), pltpu.VMEM((SUBLANES, LANES), F32)],
        compiler_params=_params("arbitrary", "arbitrary"),
        name="kv_proj",
    )(h, norm_w.reshape(1, d), w_kv, w_f, b_f, head_w.reshape(1, ATT_HEAD_DIM), tri)


BIAS_PIECES = SPLIT_PIECES


def _attn_kernel(q_ref, k_ref, v_ref, cum_ref, o_ref, kb_ref, qt_ref, vt_ref, *, tq, unroll):
    seq, hd = q_ref.shape
    nq = seq // tq
    head = pl.program_id(1)
    lane = lax.broadcasted_iota(jnp.int32, (1, hd), 1)
    cum_lane = lax.broadcasted_iota(jnp.int32, (1, cum_ref.shape[1]), 1)
    sub = lax.broadcasted_iota(jnp.int32, (hd, 1), 0)
    ones_rows = jnp.broadcast_to(jnp.where(sub < BIAS_PIECES, 1.0, 0.0), (hd, tq)).astype(BF16)

    def prepare_rows(r, carry):
        rows = pl.ds(pl.multiple_of(r * tq, tq), tq)
        col = jnp.sum(jnp.where(cum_lane == head, cum_ref[rows, :], 0.0), axis=1, keepdims=True)
        tail = jnp.zeros((tq, hd), F32)
        for piece, part in enumerate(_bf16_pieces(-LOG2E * col, BIAS_PIECES)):
            tail = jnp.where(lane == piece, part.astype(F32), tail)
        kb_ref[rows, :] = tail.astype(BF16)
        qt_ref[0:hd, rows] = q_ref[rows, :].astype(F32).T.astype(BF16)
        qt_ref[hd:2 * hd, rows] = ones_rows
        vt_ref[:, rows] = v_ref[rows, :].astype(F32).T.astype(BF16)
        return carry

    lax.fori_loop(0, nq, prepare_rows, 0)

    key_id = lax.broadcasted_iota(jnp.int32, (tq, tq), 0)
    qry_id = lax.broadcasted_iota(jnp.int32, (tq, tq), 1)
    causal = key_id <= qry_id

    def scores(qt, start):
        keys = pl.ds(start, tq)
        k_aug = jnp.concatenate([k_ref[keys, :], kb_ref[keys, :]], axis=1)
        return jnp.dot(k_aug, qt, preferred_element_type=F32)

    def update(s, start, carry, diagonal):
        m, l, acc = carry
        if diagonal:
            s = jnp.where(causal, s, -jnp.inf)
        m_new = jnp.maximum(m, jnp.max(s, axis=0, keepdims=True))
        alpha = jnp.exp2(m - m_new)
        p = jnp.exp2(s - m_new)
        l = alpha * l + jnp.sum(p, axis=0, keepdims=True)
        acc = alpha * acc + jnp.dot(vt_ref[:, pl.ds(start, tq)], p.astype(BF16),
                                    preferred_element_type=F32)
        return m_new, l, acc

    def blocks(qt, starts, carry, last_is_diagonal):
        s_next = scores(qt, starts[0])
        for u, start in enumerate(starts):
            s_cur = s_next
            if u + 1 < len(starts):
                s_next = scores(qt, starts[u + 1])
            carry = update(s_cur, start, carry, last_is_diagonal and u + 1 == len(starts))
        return carry

    for qi in range(nq):
        cols = slice(qi * tq, (qi + 1) * tq)
        qt = qt_ref[:, cols]
        carry = (jnp.full((1, tq), -jnp.inf, F32), jnp.zeros((1, tq), F32),
                 jnp.zeros((hd, tq), F32))
        n_loop = qi // unroll
        if n_loop > 0:
            def group(ki, c, qt=qt):
                base = pl.multiple_of(ki * unroll * tq, unroll * tq)
                return blocks(qt, [base + u * tq for u in range(unroll)], c, False)
            carry = lax.fori_loop(0, n_loop, group, carry)
        tail_starts = [kb * tq for kb in range(n_loop * unroll, qi + 1)]
        _, l, acc = blocks(qt, tail_starts, carry, True)
        o_ref[cols, :] = (acc / l).T.astype(o_ref.dtype)


def _attention(q, kv, cum, bsz, seq, n_heads, tq=512, unroll=4):
    t = q.shape[0]
    tq = _tile(seq, tq)
    hd = ATT_HEAD_DIM
    return pl.pallas_call(
        functools.partial(_attn_kernel, tq=tq, unroll=unroll),
        grid=(bsz, n_heads),
        in_specs=[
            pl.BlockSpec((seq, hd), lambda b, h: (b, h)),
            pl.BlockSpec((seq, hd), lambda b, h: (b, h)),
            pl.BlockSpec((seq, hd), lambda b, h: (b, n_heads + h)),
            pl.BlockSpec((seq, cum.shape[1]), lambda b, h: (b, 0)),
        ],
        out_specs=pl.BlockSpec((seq, hd), lambda b, h: (b, h)),
        out_shape=jax.ShapeDtypeStruct((t, n_heads * hd), BF16),
        scratch_shapes=[pltpu.VMEM((seq, hd), BF16),
                        pltpu.VMEM((2 * hd, seq), BF16),
                        pltpu.VMEM((hd, seq), BF16)],
        compiler_params=_params("parallel", "arbitrary"),
        name="fox_attention",
    )(q, kv, kv, cum)


def kernel(x, a_norm_w, a_in_proj, a_conv_w, a_conv_b, a_dt_bias, a_A_log, a_D, a_gnorm_w, a_out_proj, kv_norm_w, w_kvf, b_f, k_norm_w, b_norm_w, w_q, q_norm_w, w_o, ffn_norm_w, w_gate_up, w_down):
    bsz, seq, d_model = x.shape
    t = bsz * seq
    n_a = a_norm_w.shape[0]
    n_b = b_norm_w.shape[0]
    h = x.reshape(t, d_model).astype(F32)

    for i in range(n_a):
        n_ssm_heads = a_dt_bias.shape[1]
        n_main = a_in_proj.shape[2] - n_ssm_heads
        w_main = a_in_proj[i, :, :n_main].astype(BF16)
        w_dt = jnp.pad(a_in_proj[i, :, n_main:], ((0, 0), (0, LANES - n_ssm_heads))).astype(BF16)
        zx, dt_raw = _in_proj(h, a_norm_w[i], w_main, w_dt)
        y = _ssd(zx, dt_raw, a_conv_w[i], a_conv_b[i], a_dt_bias[i], a_A_log[i], a_D[i],
                 a_gnorm_w[i], bsz, seq)
        h = _mm_res(y, a_out_proj[i].astype(BF16), h, name="out_proj")
        h = _ffn(h, ffn_norm_w[i], w_gate_up[i], w_down[i])

    n_att_heads = b_f.shape[0]
    d_att = n_att_heads * ATT_HEAD_DIM
    scale = ATT_HEAD_DIM ** -0.5 * LOG2E
    kv = cum = None
    for j in range(n_b):
        if j == 0:
            w_kv = w_kvf[:, :2 * d_att].astype(BF16)
            w_f = jnp.pad(w_kvf[:, 2 * d_att:], ((0, 0), (0, LANES - n_att_heads))).astype(BF16)
            b_pad = jnp.pad(b_f.astype(F32), (0, LANES - n_att_heads)).reshape(1, LANES)
            kv, cum = _kv_proj(h, kv_norm_w, w_kv, w_f, b_pad, k_norm_w, d_att, seq)
        q = _q_proj(h, b_norm_w[j], w_q[j].astype(BF16), q_norm_w[j], scale)
        o = _attention(q, kv, cum, bsz, seq, n_att_heads)
        h = _mm_res(o, w_o[j].astype(BF16), h, name="attn_out")
        h = _ffn(h, ffn_norm_w[n_a + j], w_gate_up[n_a + j], w_down[n_a + j])

    return h.reshape(bsz, seq, d_model).astype(x.dtype)
```

```python
import functools

import jax
import jax.numpy as jnp
from jax import lax
from jax.experimental import pallas as pl
from jax.experimental.pallas import tpu as pltpu

F32 = jnp.float32
BF16 = jnp.bfloat16
EPS = 1e-6

SSM_HEAD_DIM = 64
SSM_GROUPS = 8
D_STATE = 128
CONV_WIDTH = 4
SSD_CHUNK = 128
ATT_HEAD_DIM = 128

LANES = 128
SUBLANES = 8
BF16_SUBLANES = 16
VMEM_LIMIT_BYTES = 56 * 1024 * 1024

CONV_HALO = BF16_SUBLANES
LOG2E = 1.4426950408889634

NT_DIMS = (((1,), (1,)), ((), ()))
TN_DIMS = (((0,), (0,)), ((), ()))


def _params(*semantics):
    return pltpu.CompilerParams(dimension_semantics=semantics,
                                vmem_limit_bytes=VMEM_LIMIT_BYTES)


def _silu(v):
    half = 0.5 * v
    return half + half * jnp.tanh(half)


def _softplus(v):
    return jnp.maximum(v, 0.0) + jnp.log1p(jnp.exp(-jnp.abs(v)))


def _rms_rows(v, w):
    ms = jnp.mean(v * v, axis=-1, keepdims=True)
    return v * lax.rsqrt(ms + EPS) * w


SPLIT_PIECES = 3


def _bf16_pieces(v, n):
    parts, rest = [], v
    for _ in range(n):
        part = rest.astype(BF16)
        parts.append(part)
        rest = rest - part.astype(F32)
    return parts


def _scan_matrix(rows):
    tri = jnp.arange(rows)[None, :] <= jnp.arange(rows)[:, None]
    return jnp.concatenate([tri] * SPLIT_PIECES, axis=1).astype(BF16)


def _cumsum_rows(v, scan_ref):
    stacked = jnp.concatenate(_bf16_pieces(v, SPLIT_PIECES), axis=0)
    return jnp.dot(scan_ref[...], stacked, preferred_element_type=F32)


def _tile(n, pref):
    t = min(n, pref)
    assert n % t == 0, (n, pref)
    return t


def _inproj_kernel(x_ref, nw_ref, w_ref, wdt_ref, o_ref, dt_ref, xn_ref):
    @pl.when(pl.program_id(1) == 0)
    def _():
        xn = _rms_rows(x_ref[...], nw_ref[...]).astype(BF16)
        xn_ref[...] = xn
        dt_ref[...] = jnp.dot(xn, wdt_ref[...], preferred_element_type=F32)

    o_ref[...] = jnp.dot(xn_ref[...], w_ref[...],
                         preferred_element_type=F32).astype(o_ref.dtype)


def _in_proj(h, norm_w, w_main, w_dt, tm=1024, tn=1024):
    t, d = h.shape
    n = w_main.shape[1]
    tm, tn = _tile(t, tm), _tile(n, tn)
    return pl.pallas_call(
        _inproj_kernel,
        grid=(t // tm, n // tn),
        in_specs=[
            pl.BlockSpec((tm, d), lambda i, j: (i, 0)),
            pl.BlockSpec((1, d), lambda i, j: (0, 0)),
            pl.BlockSpec((d, tn), lambda i, j: (0, j)),
            pl.BlockSpec((d, LANES), lambda i, j: (0, 0)),
        ],
        out_specs=[
            pl.BlockSpec((tm, tn), lambda i, j: (i, j)),
            pl.BlockSpec((tm, LANES), lambda i, j: (i, 0)),
        ],
        out_shape=[jax.ShapeDtypeStruct((t, n), BF16),
                   jax.ShapeDtypeStruct((t, LANES), F32)],
        scratch_shapes=[pltpu.VMEM((tm, d), BF16)],
        compiler_params=_params("parallel", "arbitrary"),
        name="in_proj",
    )(h, norm_w.reshape(1, d), w_main, w_dt)


def _ssd_kernel(z_ref, x_ref, bc_ref, dt_ref, cwx_ref, cwbc_ref, cbx_ref, cbbc_ref,
                dtb_ref, alog_ref, dskip_ref, gnw_ref, expand_ref, tri_ref, shift_ref,
                y_ref,
                state_ref, xext_ref, bcext_ref, bcs_ref):
    L = SSD_CHUNK
    P = SSM_HEAD_DIM
    N = D_STATE
    G = SSM_GROUPS
    d_inner = x_ref.shape[1]
    d_bc = bc_ref.shape[1]
    heads_per_group = d_inner // (G * P)
    gw = heads_per_group * P
    halo = CONV_HALO

    @pl.when(pl.program_id(1) == 0)
    def _():
        state_ref[...] = jnp.zeros_like(state_ref)
        xext_ref[0:halo, :] = jnp.zeros((halo, d_inner), BF16)
        bcext_ref[0:halo, :] = jnp.zeros((halo, d_bc), BF16)

    xext_ref[halo:halo + L, :] = x_ref[...]
    bcext_ref[halo:halo + L, :] = bc_ref[...]

    def conv_silu(ext_ref, cur_ref, w_ref, b_ref, cols):
        shifted = jnp.dot(shift_ref[...], ext_ref[:, cols], preferred_element_type=F32)
        acc = b_ref[:, cols] + w_ref[CONV_WIDTH - 1:CONV_WIDTH, cols] * cur_ref[:, cols].astype(F32)
        for k in range(CONV_WIDTH - 1):
            acc = acc + w_ref[k:k + 1, cols] * shifted[k * L:(k + 1) * L, :]
        return _silu(acc)

    for j in range(d_bc // gw):
        cols = slice(j * gw, (j + 1) * gw)
        bcs_ref[:, cols] = conv_silu(bcext_ref, bc_ref, cwbc_ref, cbbc_ref, cols).astype(BF16)
    bcext_ref[0:halo, :] = bcext_ref[L:L + halo, :]

    dtv = _softplus(dt_ref[...] + dtb_ref[...])
    a = dtv * (-LOG2E * jnp.exp(alog_ref[...]))
    acum = _cumsum_rows(a, tri_ref)
    a_last = acum[L - 1:L, :]
    acum_t = acum.T

    def split(v):
        return jnp.concatenate(_bf16_pieces(v, 2), axis=1)

    dt_hl = split(dtv)
    e_out_hl = split(jnp.exp2(acum))
    e_in_hl = split(dtv * jnp.exp2(a_last - acum))

    def expand(hl, cols):
        return jnp.dot(hl, expand_ref[:, cols], preferred_element_type=F32)

    row_id = lax.broadcasted_iota(jnp.int32, (L, L), 0)
    col_id = lax.broadcasted_iota(jnp.int32, (L, L), 1)
    causal = col_id <= row_id
    lane_id = lax.broadcasted_iota(jnp.int32, (L, 2 * P), 1)
    first_head = lane_id < P

    for g in range(G):
        gs = slice(g * gw, (g + 1) * gw)
        xs = conv_silu(xext_ref, x_ref, cwx_ref, cbx_ref, gs)
        xdt = (xs * expand(dt_hl, gs)).astype(BF16)
        xw = (xs * expand(e_in_hl, gs)).astype(BF16)
        e_out_x = expand(e_out_hl, gs)
        b_g = bcs_ref[:, g * N:(g + 1) * N]
        c_g = bcs_ref[:, G * N + g * N:G * N + (g + 1) * N]
        cb = lax.dot_general(c_g, b_g, NT_DIMS, preferred_element_type=F32)
        s_g = state_ref[:, gs]
        y_off = jnp.dot(c_g, s_g.astype(BF16), preferred_element_type=F32) * e_out_x
        upd = lax.dot_general(b_g, xw, TN_DIMS, preferred_element_type=F32)
        state_ref[:, gs] = s_g * e_out_x[L - 1:L, :] + upd
        y_pairs = []
        for pr in range(heads_per_group // 2):
            h0 = g * heads_per_group + 2 * pr
            ms = []
            for hh in (h0, h0 + 1):
                seg = acum[:, hh:hh + 1] - acum_t[hh:hh + 1, :]
                decay = jnp.exp2(jnp.where(causal, seg, -jnp.inf))
                ms.append((cb * decay).astype(BF16))
            m_cat = jnp.concatenate(ms, axis=1)
            xp = xdt[:, pr * 2 * P:(pr + 1) * 2 * P]
            zero = jnp.zeros_like(xp)
            rhs = jnp.concatenate([jnp.where(first_head, xp, zero),
                                   jnp.where(first_head, zero, xp)], axis=0)
            y_pairs.append(jnp.dot(m_cat, rhs, preferred_element_type=F32))
        y = jnp.concatenate(y_pairs, axis=1) + y_off + xs * dskip_ref[:, gs]
        y = y * _silu(z_ref[:, gs].astype(F32))
        y_ref[:, gs] = _rms_rows(y, gnw_ref[:, gs]).astype(y_ref.dtype)
    xext_ref[0:halo, :] = xext_ref[L:L + halo, :]


def _ssd(zx, dt_raw, conv_w, conv_b, dt_bias, a_log, d_skip, gnorm_w, bsz, seq):
    t = zx.shape[0]
    L, P, N, G = SSD_CHUNK, SSM_HEAD_DIM, D_STATE, SSM_GROUPS
    n_heads = dt_bias.shape[0]
    d_inner = n_heads * P
    d_bc = 2 * G * N
    assert zx.shape[1] == 2 * d_inner + d_bc and n_heads <= LANES
    assert d_inner % d_bc == 0 and seq % L == 0
    nc = seq // L
    bc_blk = 2 * d_inner // d_bc

    pad_h = LANES - n_heads
    dtb = jnp.pad(dt_bias.astype(F32), (0, pad_h)).reshape(1, LANES)
    alog = jnp.pad(a_log.astype(F32), (0, pad_h)).reshape(1, LANES)
    dskip_x = jnp.repeat(d_skip.astype(F32), P).reshape(1, d_inner)
    head_of_channel = jnp.arange(d_inner, dtype=jnp.int32) // P
    sel = (jnp.arange(LANES, dtype=jnp.int32)[:, None] == head_of_channel[None, :])
    expand_mat = jnp.concatenate([sel, sel], axis=0).astype(BF16)
    tri = _scan_matrix(L)
    n_shift = CONV_WIDTH - 1
    src_row = (CONV_HALO - n_shift + jnp.arange(n_shift)[:, None] + jnp.arange(L)[None, :]).reshape(-1)
    shift_mat = (src_row[:, None] == jnp.arange(CONV_HALO + L)[None, :]).astype(BF16)

    row = lambda b, c: b * nc + c
    const = lambda b, c: (0, 0)
    return pl.pallas_call(
        _ssd_kernel,
        grid=(bsz, nc),
        in_specs=[
            pl.BlockSpec((L, d_inner), lambda b, c: (row(b, c), 0)),
            pl.BlockSpec((L, d_inner), lambda b, c: (row(b, c), 1)),
            pl.BlockSpec((L, d_bc), lambda b, c: (row(b, c), bc_blk)),
            pl.BlockSpec((L, LANES), lambda b, c: (row(b, c), 0)),
            pl.BlockSpec((CONV_WIDTH, d_inner), const),
            pl.BlockSpec((CONV_WIDTH, d_bc), const),
            pl.BlockSpec((1, d_inner), const),
            pl.BlockSpec((1, d_bc), const),
            pl.BlockSpec((1, LANES), const),
            pl.BlockSpec((1, LANES), const),
            pl.BlockSpec((1, d_inner), const),
            pl.BlockSpec((1, d_inner), const),
            pl.BlockSpec((2 * LANES, d_inner), const),
            pl.BlockSpec(tri.shape, const),
            pl.BlockSpec((n_shift * L, CONV_HALO + L), const),
        ],
        out_specs=pl.BlockSpec((L, d_inner), lambda b, c: (row(b, c), 0)),
        out_shape=jax.ShapeDtypeStruct((t, d_inner), BF16),
        scratch_shapes=[
            pltpu.VMEM((N, d_inner), F32),
            pltpu.VMEM((CONV_HALO + L, d_inner), BF16),
            pltpu.VMEM((CONV_HALO + L, d_bc), BF16),
            pltpu.VMEM((L, d_bc), BF16),
        ],
        compiler_params=_params("parallel", "arbitrary"),
        name="ssd",
    )(zx, zx, zx, dt_raw,
      conv_w[:, :d_inner], conv_w[:, d_inner:],
      conv_b[:d_inner].reshape(1, d_inner), conv_b[d_inner:].reshape(1, d_bc),
      dtb, alog, dskip_x, gnorm_w.reshape(1, d_inner), expand_mat, tri, shift_mat)


def _mm_res_kernel(a_ref, w_ref, r_ref, h_ref, *norm_refs):
    h = r_ref[...] + jnp.dot(a_ref[...], w_ref[...], preferred_element_type=F32)
    h_ref[...] = h
    if norm_refs:
        hb_ref, rinv_ref = norm_refs
        hb_ref[...] = h.astype(BF16)
        ms = jnp.mean(h * h, axis=-1, keepdims=True)
        rinv_ref[...] = jnp.broadcast_to(lax.rsqrt(ms + EPS), rinv_ref.shape)


def _mm_res(a, w, res, emit_norm, tm=256, name="mm_res"):
    t, k = a.shape
    n = w.shape[1]
    tm = _tile(t, tm)
    out_specs = [pl.BlockSpec((tm, n), lambda i: (i, 0))]
    out_shape = [jax.ShapeDtypeStruct((t, n), F32)]
    if emit_norm:
        out_specs += [pl.BlockSpec((tm, n), lambda i: (i, 0)),
                      pl.BlockSpec((tm, LANES), lambda i: (i, 0))]
        out_shape += [jax.ShapeDtypeStruct((t, n), BF16),
                      jax.ShapeDtypeStruct((t, LANES), F32)]
    return pl.pallas_call(
        _mm_res_kernel,
        grid=(t // tm,),
        in_specs=[
            pl.BlockSpec((tm, k), lambda i: (i, 0)),
            pl.BlockSpec((k, n), lambda i: (0, 0), pipeline_mode=pl.Buffered(1)),
            pl.BlockSpec((tm, n), lambda i: (i, 0)),
        ],
        out_specs=out_specs,
        out_shape=out_shape,
        compiler_params=_params("arbitrary"),
        name=name,
    )(a, w, res)


def _stage_weight(w_ref, nw_ref, wb_ref):
    wb_ref[...] = (w_ref[...] * nw_ref[...]).astype(BF16)


def _scale_rows(acc, rinv):
    return jnp.concatenate([acc[:, c * LANES:(c + 1) * LANES] * rinv
                            for c in range(acc.shape[1] // LANES)], axis=1)


def _ffn_up_kernel(hb_ref, rinv_ref, nw_ref, wg_ref, wu_ref, o_ref, wgb_ref, wub_ref):
    @pl.when(pl.program_id(1) == 0)
    def _():
        _stage_weight(wg_ref, nw_ref, wgb_ref)
        _stage_weight(wu_ref, nw_ref, wub_ref)

    hb = hb_ref[...]
    rinv = rinv_ref[...]
    g = _scale_rows(jnp.dot(hb, wgb_ref[...], preferred_element_type=F32), rinv)
    u = _scale_rows(jnp.dot(hb, wub_ref[...], preferred_element_type=F32), rinv)
    o_ref[...] = (_silu(g) * u).astype(o_ref.dtype)


def _ffn_up(hb, rinv, norm_w, w_gate_up, tm=1024, tn=512):
    t, d = hb.shape
    d_ff = w_gate_up.shape[1] // 2
    tm, tn = _tile(t, tm), _tile(d_ff, tn)
    nj = d_ff // tn
    return pl.pallas_call(
        _ffn_up_kernel,
        grid=(nj, t // tm),
        in_specs=[
            pl.BlockSpec((tm, d), lambda j, i: (i, 0)),
            pl.BlockSpec((tm, LANES), lambda j, i: (i, 0)),
            pl.BlockSpec((d, 1), lambda j, i: (0, 0)),
            pl.BlockSpec((d, tn), lambda j, i: (0, j)),
            pl.BlockSpec((d, tn), lambda j, i: (0, j + nj)),
        ],
        out_specs=pl.BlockSpec((tm, tn), lambda j, i: (i, j)),
        out_shape=jax.ShapeDtypeStruct((t, d_ff), BF16),
        scratch_shapes=[pltpu.VMEM((d, tn), BF16), pltpu.VMEM((d, tn), BF16)],
        compiler_params=_params("arbitrary", "arbitrary"),
        name="ffn_up",
    )(hb, rinv, norm_w.reshape(d, 1), w_gate_up, w_gate_up)


def _ffn(h, hb, rinv, norm_w, w_gate_up, w_down, emit_norm):
    act = _ffn_up(hb, rinv, norm_w, w_gate_up)
    return _mm_res(act, w_down.astype(BF16), h, emit_norm, name="ffn_down")


def _head_norm(acc, hw, scale):
    parts = []
    for c in range(acc.shape[1] // ATT_HEAD_DIM):
        part = acc[:, c * ATT_HEAD_DIM:(c + 1) * ATT_HEAD_DIM]
        parts.append(_rms_rows(part, hw) * scale)
    return jnp.concatenate(parts, axis=1)


def _qproj_kernel(hb_ref, rinv_ref, nw_ref, w_ref, hw_ref, o_ref, wb_ref, *, scale):
    @pl.when(pl.program_id(1) == 0)
    def _():
        _stage_weight(w_ref, nw_ref, wb_ref)

    acc = _scale_rows(jnp.dot(hb_ref[...], wb_ref[...], preferred_element_type=F32), rinv_ref[...])
    o_ref[...] = _head_norm(acc, hw_ref[...], scale).astype(o_ref.dtype)


def _q_proj(hb, rinv, norm_w, w, head_w, scale, tm=1024, tn=512):
    t, d = hb.shape
    n = w.shape[1]
    tm, tn = _tile(t, tm), _tile(n, tn)
    return pl.pallas_call(
        functools.partial(_qproj_kernel, scale=scale),
        grid=(n // tn, t // tm),
        in_specs=[
            pl.BlockSpec((tm, d), lambda j, i: (i, 0)),
            pl.BlockSpec((tm, LANES), lambda j, i: (i, 0)),
            pl.BlockSpec((d, 1), lambda j, i: (0, 0)),
            pl.BlockSpec((d, tn), lambda j, i: (0, j)),
            pl.BlockSpec((1, ATT_HEAD_DIM), lambda j, i: (0, 0)),
        ],
        out_specs=pl.BlockSpec((tm, tn), lambda j, i: (i, j)),
        out_shape=jax.ShapeDtypeStruct((t, n), BF16),
        scratch_shapes=[pltpu.VMEM((d, tn), BF16)],
        compiler_params=_params("arbitrary", "arbitrary"),
        name="q_proj",
    )(hb, rinv, norm_w.reshape(d, 1), w, head_w.reshape(1, ATT_HEAD_DIM))


def _kvproj_kernel(hb_ref, rinv_ref, nw_ref, w_ref, wf_ref, bf_ref, hw_ref, tri_ref,
                   o_ref, cum_ref, wb_ref, wfb_ref, carry_ref, *, n_k_blocks, blocks_per_seq):
    j = pl.program_id(0)
    i = pl.program_id(1)
    tm = hb_ref.shape[0]

    @pl.when(i == 0)
    def _():
        _stage_weight(w_ref, nw_ref, wb_ref)

    @pl.when(j == 0)
    def _():
        @pl.when(i == 0)
        def _():
            _stage_weight(wf_ref, nw_ref, wfb_ref)

        logit = jnp.dot(hb_ref[...], wfb_ref[...], preferred_element_type=F32) * rinv_ref[...]
        log_f = -_softplus(-(logit + bf_ref[...]))

        @pl.when(i % blocks_per_seq == 0)
        def _():
            carry_ref[...] = jnp.zeros_like(carry_ref)

        rows_per_scan = tri_ref.shape[0]
        carry = carry_ref[0:1, :]
        for sb in range(tm // rows_per_scan):
            rows = slice(sb * rows_per_scan, (sb + 1) * rows_per_scan)
            cum = _cumsum_rows(log_f[rows, :], tri_ref) + carry
            cum_ref[rows, :] = cum
            carry = cum[rows_per_scan - 1:rows_per_scan, :]
        carry_ref[...] = jnp.broadcast_to(carry, carry_ref.shape)

    acc = _scale_rows(jnp.dot(hb_ref[...], wb_ref[...], preferred_element_type=F32), rinv_ref[...])

    @pl.when(j < n_k_blocks)
    def _():
        o_ref[...] = _head_norm(acc, hw_ref[...], 1.0).astype(o_ref.dtype)

    @pl.when(j >= n_k_blocks)
    def _():
        o_ref[...] = acc.astype(o_ref.dtype)


def _kv_proj(hb, rinv, norm_w, w_kvf, w_f, b_f, head_w, d_att, seq, tm=1024, tn=512, scan_rows=256):
    t, d = hb.shape
    n = 2 * d_att
    tm, tn = _tile(seq, tm), _tile(d_att, tn)
    ni = t // tm
    tri = _scan_matrix(_tile(tm, scan_rows))
    kern = functools.partial(_kvproj_kernel, n_k_blocks=d_att // tn,
                             blocks_per_seq=seq // tm)
    return pl.pallas_call(
        kern,
        grid=(n // tn, ni),
        in_specs=[
            pl.BlockSpec((tm, d), lambda j, i: (i, 0)),
            pl.BlockSpec((tm, LANES), lambda j, i: (i, 0)),
            pl.BlockSpec((d, 1), lambda j, i: (0, 0)),
            pl.BlockSpec((d, tn), lambda j, i: (0, j)),
            pl.BlockSpec((d, LANES), lambda j, i: (0, 0)),
            pl.BlockSpec((1, LANES), lambda j, i: (0, 0)),
            pl.BlockSpec((1, ATT_HEAD_DIM), lambda j, i: (0, 0)),
            pl.BlockSpec(tri.shape, lambda j, i: (0, 0)),
        ],
        out_specs=[
            pl.BlockSpec((tm, tn), lambda j, i: (i, j)),
            pl.BlockSpec((tm, LANES), lambda j, i: (jnp.where(j == 0, i, ni - 1), 0)),
        ],
        out_shape=[jax.ShapeDtypeStruct((t, n), BF16),
                   jax.ShapeDtypeStruct((t, LANES), F32)],
        scratch_shapes=[pltpu.VMEM((d, tn), BF16), pltpu.VMEM((d, LANES), BF16),
                        pltpu.VMEM((SUBLANES, LANES), F32)],
        compiler_params=_params("arbitrary", "arbitrary"),
        name="kv_proj",
    )(hb, rinv, norm_w.reshape(d, 1), w_kvf, w_f, b_f, head_w.reshape(1, ATT_HEAD_DIM), tri)


BIAS_PIECES = SPLIT_PIECES


def _attn_kernel(q_ref, k_ref, v_ref, cum_ref, o_ref, kb_ref, qt_ref, vt_ref, *, tq, unroll):
    seq, hd = q_ref.shape
    nq = seq // tq
    head = pl.program_id(1)
    lane = lax.broadcasted_iota(jnp.int32, (1, hd), 1)
    cum_lane = lax.broadcasted_iota(jnp.int32, (1, cum_ref.shape[1]), 1)
    sub = lax.broadcasted_iota(jnp.int32, (hd, 1), 0)
    ones_rows = jnp.broadcast_to(jnp.where(sub < BIAS_PIECES, 1.0, 0.0), (hd, tq)).astype(BF16)

    def prepare_rows(r, carry):
        rows = pl.ds(pl.multiple_of(r * tq, tq), tq)
        col = jnp.sum(jnp.where(cum_lane == head, cum_ref[rows, :], 0.0), axis=1, keepdims=True)
        tail = jnp.zeros((tq, hd), F32)
        for piece, part in enumerate(_bf16_pieces(-LOG2E * col, BIAS_PIECES)):
            tail = jnp.where(lane == piece, part.astype(F32), tail)
        kb_ref[rows, :] = tail.astype(BF16)
        qt_ref[0:hd, rows] = q_ref[rows, :].astype(F32).T.astype(BF16)
        qt_ref[hd:2 * hd, rows] = ones_rows
        vt_ref[:, rows] = v_ref[rows, :].astype(F32).T.astype(BF16)
        return carry

    lax.fori_loop(0, nq, prepare_rows, 0)

    key_id = lax.broadcasted_iota(jnp.int32, (tq, tq), 0)
    qry_id = lax.broadcasted_iota(jnp.int32, (tq, tq), 1)
    causal = key_id <= qry_id

    def scores(qt, start):
        keys = pl.ds(start, tq)
        k_aug = jnp.concatenate([k_ref[keys, :], kb_ref[keys, :]], axis=1)
        return jnp.dot(k_aug, qt, preferred_element_type=F32)

    def update(s, start, carry, diagonal):
        m, l, acc = carry
        if diagonal:
            s = jnp.where(causal, s, -jnp.inf)
        m_new = jnp.maximum(m, jnp.max(s, axis=0, keepdims=True))
        alpha = jnp.exp2(m - m_new)
        p = jnp.exp2(s - m_new)
        l = alpha * l + jnp.sum(p, axis=0, keepdims=True)
        acc = alpha * acc + jnp.dot(vt_ref[:, pl.ds(start, tq)], p.astype(BF16),
                                    preferred_element_type=F32)
        return m_new, l, acc

    def blocks(qt, starts, carry, last_is_diagonal):
        s_next = scores(qt, starts[0])
        for u, start in enumerate(starts):
            s_cur = s_next
            if u + 1 < len(starts):
                s_next = scores(qt, starts[u + 1])
            carry = update(s_cur, start, carry, last_is_diagonal and u + 1 == len(starts))
        return carry

    for qi in range(nq):
        cols = slice(qi * tq, (qi + 1) * tq)
        qt = qt_ref[:, cols]
        carry = (jnp.full((1, tq), -jnp.inf, F32), jnp.zeros((1, tq), F32),
                 jnp.zeros((hd, tq), F32))
        n_loop = qi // unroll
        if n_loop > 0:
            def group(ki, c, qt=qt):
                base = pl.multiple_of(ki * unroll * tq, unroll * tq)
                return blocks(qt, [base + u * tq for u in range(unroll)], c, False)
            carry = lax.fori_loop(0, n_loop, group, carry)
        tail_starts = [kb * tq for kb in range(n_loop * unroll, qi + 1)]
        _, l, acc = blocks(qt, tail_starts, carry, True)
        o_ref[cols, :] = (acc / l).T.astype(o_ref.dtype)


def _attention(q, kv, cum, bsz, seq, n_heads, tq=512, unroll=4):
    t = q.shape[0]
    tq = _tile(seq, tq)
    hd = ATT_HEAD_DIM
    return pl.pallas_call(
        functools.partial(_attn_kernel, tq=tq, unroll=unroll),
        grid=(bsz, n_heads),
        in_specs=[
            pl.BlockSpec((seq, hd), lambda b, h: (b, h)),
            pl.BlockSpec((seq, hd), lambda b, h: (b, h)),
            pl.BlockSpec((seq, hd), lambda b, h: (b, n_heads + h)),
            pl.BlockSpec((seq, cum.shape[1]), lambda b, h: (b, 0)),
        ],
        out_specs=pl.BlockSpec((seq, hd), lambda b, h: (b, h)),
        out_shape=jax.ShapeDtypeStruct((t, n_heads * hd), BF16),
        scratch_shapes=[pltpu.VMEM((seq, hd), BF16),
                        pltpu.VMEM((2 * hd, seq), BF16),
                        pltpu.VMEM((hd, seq), BF16)],
        compiler_params=_params("parallel", "arbitrary"),
        name="fox_attention",
    )(q, kv, kv, cum)


def kernel(x, a_norm_w, a_in_proj, a_conv_w, a_conv_b, a_dt_bias, a_A_log, a_D, a_gnorm_w, a_out_proj, kv_norm_w, w_kvf, b_f, k_norm_w, b_norm_w, w_q, q_norm_w, w_o, ffn_norm_w, w_gate_up, w_down):
    bsz, seq, d_model = x.shape
    t = bsz * seq
    n_a = a_norm_w.shape[0]
    n_b = b_norm_w.shape[0]
    depth = n_a + n_b
    assert n_a >= 1, "the attention layers take the bf16 stream a Mamba-2 layer emits"
    h = x.reshape(t, d_model).astype(F32)
    hb = rinv = None

    for i in range(n_a):
        n_ssm_heads = a_dt_bias.shape[1]
        n_main = a_in_proj.shape[2] - n_ssm_heads
        w_main = a_in_proj[i, :, :n_main].astype(BF16)
        w_dt = jnp.pad(a_in_proj[i, :, n_main:], ((0, 0), (0, LANES - n_ssm_heads))).astype(BF16)
        zx, dt_raw = _in_proj(h, a_norm_w[i], w_main, w_dt)
        y = _ssd(zx, dt_raw, a_conv_w[i], a_conv_b[i], a_dt_bias[i], a_A_log[i], a_D[i],
                 a_gnorm_w[i], bsz, seq)
        h, hb, rinv = _mm_res(y, a_out_proj[i].astype(BF16), h, True, name="out_proj")
        more = i + 1 < depth
        h, *norm = _ffn(h, hb, rinv, ffn_norm_w[i], w_gate_up[i], w_down[i], more)
        hb, rinv = norm if more else (None, None)

    n_att_heads = b_f.shape[0]
    d_att = n_att_heads * ATT_HEAD_DIM
    scale = ATT_HEAD_DIM ** -0.5 * LOG2E
    kv = cum = None
    for j in range(n_b):
        if j == 0:
            w_f = jnp.pad(w_kvf[:, 2 * d_att:], ((0, 0), (0, LANES - n_att_heads)))
            b_pad = jnp.pad(b_f.astype(F32), (0, LANES - n_att_heads)).reshape(1, LANES)
            kv, cum = _kv_proj(hb, rinv, kv_norm_w, w_kvf, w_f, b_pad, k_norm_w, d_att, seq)
        q = _q_proj(hb, rinv, b_norm_w[j], w_q[j], q_norm_w[j], scale)
        o = _attention(q, kv, cum, bsz, seq, n_att_heads)
        h, hb, rinv = _mm_res(o, w_o[j].astype(BF16), h, True, name="attn_out")
        more = n_a + j + 1 < depth
        h, *norm = _ffn(h, hb, rinv, ffn_norm_w[n_a + j], w_gate_up[n_a + j], w_down[n_a + j], more)
        hb, rinv = norm if more else (None, None)

    return h.reshape(bsz, seq, d_model).astype(x.dtype)
```

```python
import functools

import jax
import jax.numpy as jnp
from jax import lax
from jax.experimental import pallas as pl
from jax.experimental.pallas import tpu as pltpu

F32 = jnp.float32
BF16 = jnp.bfloat16
EPS = 1e-6

SSM_HEAD_DIM = 64
SSM_GROUPS = 8
D_STATE = 128
CONV_WIDTH = 4
SSD_CHUNK = 128
ATT_HEAD_DIM = 128

LANES = 128
SUBLANES = 8
BF16_SUBLANES = 16
VMEM_LIMIT_BYTES = 56 * 1024 * 1024

CONV_HALO = BF16_SUBLANES
LOG2E = 1.4426950408889634

NT_DIMS = (((1,), (1,)), ((), ()))
TN_DIMS = (((0,), (0,)), ((), ()))


def _params(*semantics):
    return pltpu.CompilerParams(dimension_semantics=semantics,
                                vmem_limit_bytes=VMEM_LIMIT_BYTES)


def _silu(v):
    half = 0.5 * v
    return half + half * jnp.tanh(half)


def _softplus(v):
    return jnp.maximum(v, 0.0) + jnp.log1p(jnp.exp(-jnp.abs(v)))


def _rms_rows(v, w):
    ms = jnp.mean(v * v, axis=-1, keepdims=True)
    return v * lax.rsqrt(ms + EPS) * w


SPLIT_PIECES = 3


def _bf16_pieces(v, n):
    parts, rest = [], v
    for _ in range(n):
        part = rest.astype(BF16)
        parts.append(part)
        rest = rest - part.astype(F32)
    return parts


def _scan_matrix(rows):
    tri = jnp.arange(rows)[None, :] <= jnp.arange(rows)[:, None]
    return jnp.concatenate([tri] * SPLIT_PIECES, axis=1).astype(BF16)


def _cumsum_rows(v, scan_ref):
    stacked = jnp.concatenate(_bf16_pieces(v, SPLIT_PIECES), axis=0)
    return jnp.dot(scan_ref[...], stacked, preferred_element_type=F32)


def _tile(n, pref):
    t = min(n, pref)
    assert n % t == 0, (n, pref)
    return t


def _inproj_kernel(x_ref, nw_ref, w_ref, wdt_ref, o_ref, dt_ref, xn_ref):
    @pl.when(pl.program_id(1) == 0)
    def _():
        xn = _rms_rows(x_ref[...], nw_ref[...]).astype(BF16)
        xn_ref[...] = xn
        dt_ref[...] = jnp.dot(xn, wdt_ref[...].astype(BF16), preferred_element_type=F32)

    o_ref[...] = jnp.dot(xn_ref[...], w_ref[...],
                         preferred_element_type=F32).astype(o_ref.dtype)


def _in_proj(h, norm_w, w_main, w_dt, tm=1024, tn=1024):
    t, d = h.shape
    n = w_main.shape[1]
    tm, tn = _tile(t, tm), _tile(n, tn)
    return pl.pallas_call(
        _inproj_kernel,
        grid=(t // tm, n // tn),
        in_specs=[
            pl.BlockSpec((tm, d), lambda i, j: (i, 0)),
            pl.BlockSpec((1, d), lambda i, j: (0, 0)),
            pl.BlockSpec((d, tn), lambda i, j: (0, j)),
            pl.BlockSpec((d, LANES), lambda i, j: (0, 0)),
        ],
        out_specs=[
            pl.BlockSpec((tm, tn), lambda i, j: (i, j)),
            pl.BlockSpec((tm, LANES), lambda i, j: (i, 0)),
        ],
        out_shape=[jax.ShapeDtypeStruct((t, n), BF16),
                   jax.ShapeDtypeStruct((t, LANES), F32)],
        scratch_shapes=[pltpu.VMEM((tm, d), BF16)],
        compiler_params=_params("parallel", "arbitrary"),
        name="in_proj",
    )(h, norm_w.reshape(1, d), w_main, w_dt)


def _ssd_kernel(z_ref, x_ref, bc_ref, dt_ref, cwx_ref, cwbc_ref, cbx_ref, cbbc_ref,
                dtb_ref, alog_ref, dskip_ref, gnw_ref, expand_ref, tri_ref, shift_ref,
                y_ref,
                state_ref, xext_ref, bcext_ref, bcs_ref):
    L = SSD_CHUNK
    P = SSM_HEAD_DIM
    N = D_STATE
    G = SSM_GROUPS
    d_inner = x_ref.shape[1]
    d_bc = bc_ref.shape[1]
    heads_per_group = d_inner // (G * P)
    gw = heads_per_group * P
    halo = CONV_HALO

    @pl.when(pl.program_id(1) == 0)
    def _():
        state_ref[...] = jnp.zeros_like(state_ref)
        xext_ref[0:halo, :] = jnp.zeros((halo, d_inner), BF16)
        bcext_ref[0:halo, :] = jnp.zeros((halo, d_bc), BF16)

    xext_ref[halo:halo + L, :] = x_ref[...]
    bcext_ref[halo:halo + L, :] = bc_ref[...]

    def conv_silu(ext_ref, cur_ref, w_ref, b_ref, cols):
        shifted = jnp.dot(shift_ref[...], ext_ref[:, cols], preferred_element_type=F32)
        acc = b_ref[:, cols] + w_ref[CONV_WIDTH - 1:CONV_WIDTH, cols] * cur_ref[:, cols].astype(F32)
        for k in range(CONV_WIDTH - 1):
            acc = acc + w_ref[k:k + 1, cols] * shifted[k * L:(k + 1) * L, :]
        return _silu(acc)

    for j in range(d_bc // gw):
        cols = slice(j * gw, (j + 1) * gw)
        bcs_ref[:, cols] = conv_silu(bcext_ref, bc_ref, cwbc_ref, cbbc_ref, cols).astype(BF16)
    bcext_ref[0:halo, :] = bcext_ref[L:L + halo, :]

    dtv = _softplus(dt_ref[...] + dtb_ref[...])
    a = dtv * (-LOG2E * jnp.exp(alog_ref[...]))
    acum = _cumsum_rows(a, tri_ref)
    a_last = acum[L - 1:L, :]
    acum_t = acum.T

    def split(v):
        return jnp.concatenate(_bf16_pieces(v, 2), axis=1)

    dt_hl = split(dtv)
    e_out_hl = split(jnp.exp2(acum))
    e_in_hl = split(dtv * jnp.exp2(a_last - acum))

    def expand(hl, cols):
        return jnp.dot(hl, expand_ref[:, cols], preferred_element_type=F32)

    row_id = lax.broadcasted_iota(jnp.int32, (L, L), 0)
    col_id = lax.broadcasted_iota(jnp.int32, (L, L), 1)
    causal = col_id <= row_id
    lane_id = lax.broadcasted_iota(jnp.int32, (L, 2 * P), 1)
    first_head = lane_id < P

    for g in range(G):
        gs = slice(g * gw, (g + 1) * gw)
        xs = conv_silu(xext_ref, x_ref, cwx_ref, cbx_ref, gs)
        xdt = (xs * expand(dt_hl, gs)).astype(BF16)
        xw = (xs * expand(e_in_hl, gs)).astype(BF16)
        e_out_x = expand(e_out_hl, gs)
        b_g = bcs_ref[:, g * N:(g + 1) * N]
        c_g = bcs_ref[:, G * N + g * N:G * N + (g + 1) * N]
        cb = lax.dot_general(c_g, b_g, NT_DIMS, preferred_element_type=F32)
        s_g = state_ref[:, gs]
        y_off = jnp.dot(c_g, s_g.astype(BF16), preferred_element_type=F32) * e_out_x
        upd = lax.dot_general(b_g, xw, TN_DIMS, preferred_element_type=F32)
        state_ref[:, gs] = s_g * e_out_x[L - 1:L, :] + upd
        y_pairs = []
        for pr in range(heads_per_group // 2):
            h0 = g * heads_per_group + 2 * pr
            ms = []
            for hh in (h0, h0 + 1):
                seg = acum[:, hh:hh + 1] - acum_t[hh:hh + 1, :]
                decay = jnp.exp2(jnp.where(causal, seg, -jnp.inf))
                ms.append((cb * decay).astype(BF16))
            m_cat = jnp.concatenate(ms, axis=1)
            xp = xdt[:, pr * 2 * P:(pr + 1) * 2 * P]
            zero = jnp.zeros_like(xp)
            rhs = jnp.concatenate([jnp.where(first_head, xp, zero),
                                   jnp.where(first_head, zero, xp)], axis=0)
            y_pairs.append(jnp.dot(m_cat, rhs, preferred_element_type=F32))
        y = jnp.concatenate(y_pairs, axis=1) + y_off + xs * dskip_ref[:, gs]
        y = y * _silu(z_ref[:, gs].astype(F32))
        y_ref[:, gs] = _rms_rows(y, gnw_ref[:, gs]).astype(y_ref.dtype)
    xext_ref[0:halo, :] = xext_ref[L:L + halo, :]


def _ssd(zx, dt_raw, conv_w, conv_b, dt_bias, a_log, d_skip, gnorm_w, bsz, seq):
    t = zx.shape[0]
    L, P, N, G = SSD_CHUNK, SSM_HEAD_DIM, D_STATE, SSM_GROUPS
    n_heads = dt_bias.shape[0]
    d_inner = n_heads * P
    d_bc = 2 * G * N
    assert zx.shape[1] == 2 * d_inner + d_bc and n_heads <= LANES
    assert d_inner % d_bc == 0 and seq % L == 0
    nc = seq // L
    bc_blk = 2 * d_inner // d_bc

    pad_h = LANES - n_heads
    dtb = jnp.pad(dt_bias.astype(F32), (0, pad_h)).reshape(1, LANES)
    alog = jnp.pad(a_log.astype(F32), (0, pad_h)).reshape(1, LANES)
    dskip_x = jnp.repeat(d_skip.astype(F32), P).reshape(1, d_inner)
    head_of_channel = jnp.arange(d_inner, dtype=jnp.int32) // P
    sel = (jnp.arange(LANES, dtype=jnp.int32)[:, None] == head_of_channel[None, :])
    expand_mat = jnp.concatenate([sel, sel], axis=0).astype(BF16)
    tri = _scan_matrix(L)
    n_shift = CONV_WIDTH - 1
    src_row = (CONV_HALO - n_shift + jnp.arange(n_shift)[:, None] + jnp.arange(L)[None, :]).reshape(-1)
    shift_mat = (src_row[:, None] == jnp.arange(CONV_HALO + L)[None, :]).astype(BF16)

    row = lambda b, c: b * nc + c
    const = lambda b, c: (0, 0)
    return pl.pallas_call(
        _ssd_kernel,
        grid=(bsz, nc),
        in_specs=[
            pl.BlockSpec((L, d_inner), lambda b, c: (row(b, c), 0)),
            pl.BlockSpec((L, d_inner), lambda b, c: (row(b, c), 1)),
            pl.BlockSpec((L, d_bc), lambda b, c: (row(b, c), bc_blk)),
            pl.BlockSpec((L, LANES), lambda b, c: (row(b, c), 0)),
            pl.BlockSpec((CONV_WIDTH, d_inner), const),
            pl.BlockSpec((CONV_WIDTH, d_bc), const),
            pl.BlockSpec((1, d_inner), const),
            pl.BlockSpec((1, d_bc), const),
            pl.BlockSpec((1, LANES), const),
            pl.BlockSpec((1, LANES), const),
            pl.BlockSpec((1, d_inner), const),
            pl.BlockSpec((1, d_inner), const),
            pl.BlockSpec((2 * LANES, d_inner), const),
            pl.BlockSpec(tri.shape, const),
            pl.BlockSpec((n_shift * L, CONV_HALO + L), const),
        ],
        out_specs=pl.BlockSpec((L, d_inner), lambda b, c: (row(b, c), 0)),
        out_shape=jax.ShapeDtypeStruct((t, d_inner), BF16),
        scratch_shapes=[
            pltpu.VMEM((N, d_inner), F32),
            pltpu.VMEM((CONV_HALO + L, d_inner), BF16),
            pltpu.VMEM((CONV_HALO + L, d_bc), BF16),
            pltpu.VMEM((L, d_bc), BF16),
        ],
        compiler_params=_params("parallel", "arbitrary"),
        name="ssd",
    )(zx, zx, zx, dt_raw,
      conv_w[:, :d_inner], conv_w[:, d_inner:],
      conv_b[:d_inner].reshape(1, d_inner), conv_b[d_inner:].reshape(1, d_bc),
      dtb, alog, dskip_x, gnorm_w.reshape(1, d_inner), expand_mat, tri, shift_mat)


def _mm_res_kernel(a_ref, w_ref, r_ref, h_ref, *norm_refs):
    h = r_ref[...] + jnp.dot(a_ref[...], w_ref[...], preferred_element_type=F32)
    h_ref[...] = h
    if norm_refs:
        hb_ref, rinv_ref = norm_refs
        hb_ref[...] = h.astype(BF16)
        ms = jnp.mean(h * h, axis=-1, keepdims=True)
        rinv_ref[...] = jnp.broadcast_to(lax.rsqrt(ms + EPS), rinv_ref.shape)


def _mm_res(a, w_layers, layer, res, emit_norm, tm=256, name="mm_res"):
    t, k = a.shape
    n = w_layers.shape[2]
    tm = _tile(t, tm)
    out_specs = [pl.BlockSpec((tm, n), lambda i: (i, 0))]
    out_shape = [jax.ShapeDtypeStruct((t, n), F32)]
    if emit_norm:
        out_specs += [pl.BlockSpec((tm, n), lambda i: (i, 0)),
                      pl.BlockSpec((tm, LANES), lambda i: (i, 0))]
        out_shape += [jax.ShapeDtypeStruct((t, n), BF16),
                      jax.ShapeDtypeStruct((t, LANES), F32)]
    return pl.pallas_call(
        _mm_res_kernel,
        grid=(t // tm,),
        in_specs=[
            pl.BlockSpec((tm, k), lambda i: (i, 0)),
            pl.BlockSpec((None, k, n), lambda i: (layer, 0, 0), pipeline_mode=pl.Buffered(1)),
            pl.BlockSpec((tm, n), lambda i: (i, 0)),
        ],
        out_specs=out_specs,
        out_shape=out_shape,
        compiler_params=_params("arbitrary"),
        name=name,
    )(a, w_layers, res)


def _stage_weight(w_ref, nw_ref, wb_ref):
    wb_ref[...] = (w_ref[...] * nw_ref[...]).astype(BF16)


def _scale_rows(acc, rinv):
    return jnp.concatenate([acc[:, c * LANES:(c + 1) * LANES] * rinv
                            for c in range(acc.shape[1] // LANES)], axis=1)


def _ffn_up_kernel(hb_ref, rinv_ref, nw_ref, wg_ref, wu_ref, o_ref, wgb_ref, wub_ref):
    @pl.when(pl.program_id(1) == 0)
    def _():
        _stage_weight(wg_ref, nw_ref, wgb_ref)
        _stage_weight(wu_ref, nw_ref, wub_ref)

    hb = hb_ref[...]
    rinv = rinv_ref[...]
    g = _scale_rows(jnp.dot(hb, wgb_ref[...], preferred_element_type=F32), rinv)
    u = _scale_rows(jnp.dot(hb, wub_ref[...], preferred_element_type=F32), rinv)
    o_ref[...] = (_silu(g) * u).astype(o_ref.dtype)


def _ffn_up(hb, rinv, norm_w, w_gate_up, layer, tm=1024, tn=512):
    t, d = hb.shape
    d_ff = w_gate_up.shape[2] // 2
    tm, tn = _tile(t, tm), _tile(d_ff, tn)
    nj = d_ff // tn
    return pl.pallas_call(
        _ffn_up_kernel,
        grid=(nj, t // tm),
        in_specs=[
            pl.BlockSpec((tm, d), lambda j, i: (i, 0)),
            pl.BlockSpec((tm, LANES), lambda j, i: (i, 0)),
            pl.BlockSpec((d, 1), lambda j, i: (0, 0)),
            pl.BlockSpec((None, d, tn), lambda j, i: (layer, 0, j)),
            pl.BlockSpec((None, d, tn), lambda j, i: (layer, 0, j + nj)),
        ],
        out_specs=pl.BlockSpec((tm, tn), lambda j, i: (i, j)),
        out_shape=jax.ShapeDtypeStruct((t, d_ff), BF16),
        scratch_shapes=[pltpu.VMEM((d, tn), BF16), pltpu.VMEM((d, tn), BF16)],
        compiler_params=_params("arbitrary", "arbitrary"),
        name="ffn_up",
    )(hb, rinv, norm_w.reshape(d, 1), w_gate_up, w_gate_up)


def _ffn(h, hb, rinv, norm_w, w_gate_up, w_down_b, layer, emit_norm):
    act = _ffn_up(hb, rinv, norm_w, w_gate_up, layer)
    return _mm_res(act, w_down_b, layer, h, emit_norm, name="ffn_down")


def _head_norm(acc, hw, scale):
    parts = []
    for c in range(acc.shape[1] // ATT_HEAD_DIM):
        part = acc[:, c * ATT_HEAD_DIM:(c + 1) * ATT_HEAD_DIM]
        parts.append(_rms_rows(part, hw) * scale)
    return jnp.concatenate(parts, axis=1)


def _qproj_kernel(hb_ref, rinv_ref, nw_ref, w_ref, hw_ref, o_ref, wb_ref, *, scale):
    @pl.when(pl.program_id(1) == 0)
    def _():
        _stage_weight(w_ref, nw_ref, wb_ref)

    acc = _scale_rows(jnp.dot(hb_ref[...], wb_ref[...], preferred_element_type=F32), rinv_ref[...])
    o_ref[...] = _head_norm(acc, hw_ref[...], scale).T.astype(o_ref.dtype)


def _q_proj(hb, rinv, norm_w, w_layers, layer, head_w, scale, tm=1024, tn=512):
    t, d = hb.shape
    n = w_layers.shape[2]
    tm, tn = _tile(t, tm), _tile(n, tn)
    return pl.pallas_call(
        functools.partial(_qproj_kernel, scale=scale),
        grid=(n // tn, t // tm),
        in_specs=[
            pl.BlockSpec((tm, d), lambda j, i: (i, 0)),
            pl.BlockSpec((tm, LANES), lambda j, i: (i, 0)),
            pl.BlockSpec((d, 1), lambda j, i: (0, 0)),
            pl.BlockSpec((None, d, tn), lambda j, i: (layer, 0, j)),
            pl.BlockSpec((1, ATT_HEAD_DIM), lambda j, i: (0, 0)),
        ],
        out_specs=pl.BlockSpec((tn, tm), lambda j, i: (j, i)),
        out_shape=jax.ShapeDtypeStruct((n, t), BF16),
        scratch_shapes=[pltpu.VMEM((d, tn), BF16)],
        compiler_params=_params("arbitrary", "arbitrary"),
        name="q_proj",
    )(hb, rinv, norm_w.reshape(d, 1), w_layers, head_w.reshape(1, ATT_HEAD_DIM))


def _kvproj_kernel(hb_ref, rinv_ref, nw_ref, w_ref, wf_ref, bf_ref, hw_ref, tri_ref,
                   k_ref, vt_ref, cum_ref, wb_ref, wfb_ref, carry_ref, *, n_k_blocks, blocks_per_seq):
    j = pl.program_id(0)
    i = pl.program_id(1)
    tm = hb_ref.shape[0]

    @pl.when(i == 0)
    def _():
        _stage_weight(w_ref, nw_ref, wb_ref)

    @pl.when(j == 0)
    def _():
        @pl.when(i == 0)
        def _():
            _stage_weight(wf_ref, nw_ref, wfb_ref)

        logit = jnp.dot(hb_ref[...], wfb_ref[...], preferred_element_type=F32) * rinv_ref[...]
        log_f = -_softplus(-(logit + bf_ref[...]))

        @pl.when(i % blocks_per_seq == 0)
        def _():
            carry_ref[...] = jnp.zeros_like(carry_ref)

        rows_per_scan = tri_ref.shape[0]
        carry = carry_ref[0:1, :]
        for sb in range(tm // rows_per_scan):
            rows = slice(sb * rows_per_scan, (sb + 1) * rows_per_scan)
            cum = _cumsum_rows(log_f[rows, :], tri_ref) + carry
            cum_ref[rows, :] = jnp.concatenate(_bf16_pieces(-LOG2E * cum, BIAS_PIECES), axis=1)
            carry = cum[rows_per_scan - 1:rows_per_scan, :]
        carry_ref[...] = jnp.broadcast_to(carry, carry_ref.shape)

    acc = _scale_rows(jnp.dot(hb_ref[...], wb_ref[...], preferred_element_type=F32), rinv_ref[...])

    @pl.when(j < n_k_blocks)
    def _():
        k_ref[...] = _head_norm(acc, hw_ref[...], 1.0).astype(k_ref.dtype)

    @pl.when(j >= n_k_blocks)
    def _():
        vt_ref[...] = acc.T.astype(vt_ref.dtype)


def _kv_proj(hb, rinv, norm_w, w_kvf, w_f, b_f, head_w, d_att, seq, tm=1024, tn=512, scan_rows=256):
    t, d = hb.shape
    n = 2 * d_att
    tm, tn = _tile(seq, tm), _tile(d_att, tn)
    ni = t // tm
    nk = d_att // tn
    tri = _scan_matrix(_tile(tm, scan_rows))
    kern = functools.partial(_kvproj_kernel, n_k_blocks=nk, blocks_per_seq=seq // tm)
    return pl.pallas_call(
        kern,
        grid=(n // tn, ni),
        in_specs=[
            pl.BlockSpec((tm, d), lambda j, i: (i, 0)),
            pl.BlockSpec((tm, LANES), lambda j, i: (i, 0)),
            pl.BlockSpec((d, 1), lambda j, i: (0, 0)),
            pl.BlockSpec((d, tn), lambda j, i: (0, j)),
            pl.BlockSpec((d, LANES), lambda j, i: (0, 0)),
            pl.BlockSpec((1, LANES), lambda j, i: (0, 0)),
            pl.BlockSpec((1, ATT_HEAD_DIM), lambda j, i: (0, 0)),
            pl.BlockSpec(tri.shape, lambda j, i: (0, 0)),
        ],
        out_specs=[
            pl.BlockSpec((tm, tn), lambda j, i: (jnp.where(j < nk, i, ni - 1), jnp.minimum(j, nk - 1))),
            pl.BlockSpec((tn, tm), lambda j, i: (jnp.maximum(j - nk, 0), jnp.where(j < nk, 0, i))),
            pl.BlockSpec((tm, BIAS_PIECES * LANES), lambda j, i: (jnp.where(j == 0, i, ni - 1), 0)),
        ],
        out_shape=[jax.ShapeDtypeStruct((t, d_att), BF16),
                   jax.ShapeDtypeStruct((d_att, t), BF16),
                   jax.ShapeDtypeStruct((t, BIAS_PIECES * LANES), BF16)],
        scratch_shapes=[pltpu.VMEM((d, tn), BF16), pltpu.VMEM((d, LANES), BF16),
                        pltpu.VMEM((SUBLANES, LANES), F32)],
        compiler_params=_params("arbitrary", "arbitrary"),
        name="kv_proj",
    )(hb, rinv, norm_w.reshape(d, 1), w_kvf, w_f, b_f, head_w.reshape(1, ATT_HEAD_DIM), tri)


BIAS_PIECES = SPLIT_PIECES


def _attn_kernel(qt_ref, k_ref, vt_ref, cum_ref, o_ref, kb_ref, *, tq, unroll):
    seq, hd = k_ref.shape
    nq = seq // tq
    head = pl.program_id(1)
    sub = lax.broadcasted_iota(jnp.int32, (hd, 1), 0)
    ones_rows = jnp.broadcast_to(jnp.where(sub < BIAS_PIECES, 1.0, 0.0), (hd, tq)).astype(BF16)

    @pl.when(head == 0)
    def _():
        n_heads = kb_ref.shape[0]
        src = lax.broadcasted_iota(jnp.int32, (cum_ref.shape[1], n_heads * hd), 0)
        dst = lax.broadcasted_iota(jnp.int32, (cum_ref.shape[1], n_heads * hd), 1)
        dst_head, dst_lane = dst // hd, dst % hd
        pick = jnp.where((src == dst_lane * LANES + dst_head) & (dst_lane < BIAS_PIECES),
                         1.0, 0.0).astype(BF16)

        def prepare_rows(r, carry):
            rows = pl.ds(pl.multiple_of(r * tq, tq), tq)
            placed = jnp.dot(cum_ref[rows, :], pick, preferred_element_type=F32).astype(BF16)
            for hh in range(n_heads):
                kb_ref[hh, rows, :] = placed[:, hh * hd:(hh + 1) * hd]
            return carry

        lax.fori_loop(0, nq, prepare_rows, 0)

    key_id = lax.broadcasted_iota(jnp.int32, (tq, tq), 0)
    qry_id = lax.broadcasted_iota(jnp.int32, (tq, tq), 1)
    causal = key_id <= qry_id

    def scores(qt, start):
        keys = pl.ds(start, tq)
        k_aug = jnp.concatenate([k_ref[keys, :], kb_ref[head, keys, :]], axis=1)
        return jnp.dot(k_aug, qt, preferred_element_type=F32)

    def update(s, start, carry, diagonal):
        m, l, acc = carry
        if diagonal:
            s = jnp.where(causal, s, -jnp.inf)
        m_new = jnp.maximum(m, jnp.max(s, axis=0, keepdims=True))
        alpha = jnp.exp2(m - m_new)
        p = jnp.exp2(s - m_new)
        l = alpha * l + jnp.sum(p, axis=0, keepdims=True)
        acc = alpha * acc + jnp.dot(vt_ref[:, pl.ds(start, tq)], p.astype(BF16),
                                    preferred_element_type=F32)
        return m_new, l, acc

    def blocks(qt, starts, carry, last_is_diagonal):
        s_next = scores(qt, starts[0])
        for u, start in enumerate(starts):
            s_cur = s_next
            if u + 1 < len(starts):
                s_next = scores(qt, starts[u + 1])
            carry = update(s_cur, start, carry, last_is_diagonal and u + 1 == len(starts))
        return carry

    for qi in range(nq):
        cols = slice(qi * tq, (qi + 1) * tq)
        qt = jnp.concatenate([qt_ref[:, cols], ones_rows], axis=0)
        carry = (jnp.full((1, tq), -jnp.inf, F32), jnp.zeros((1, tq), F32),
                 jnp.zeros((hd, tq), F32))
        n_loop = qi // unroll
        if n_loop > 0:
            def group(ki, c, qt=qt):
                base = pl.multiple_of(ki * unroll * tq, unroll * tq)
                return blocks(qt, [base + u * tq for u in range(unroll)], c, False)
            carry = lax.fori_loop(0, n_loop, group, carry)
        tail_starts = [kb * tq for kb in range(n_loop * unroll, qi + 1)]
        _, l, acc = blocks(qt, tail_starts, carry, True)
        o_ref[cols, :] = (acc / l).T.astype(o_ref.dtype)


def _attention(q_t, k, v_t, cum, bsz, seq, n_heads, tq=512, unroll=4):
    t = k.shape[0]
    tq = _tile(seq, tq)
    hd = ATT_HEAD_DIM
    return pl.pallas_call(
        functools.partial(_attn_kernel, tq=tq, unroll=unroll),
        grid=(bsz, n_heads),
        in_specs=[
            pl.BlockSpec((hd, seq), lambda b, h: (h, b)),
            pl.BlockSpec((seq, hd), lambda b, h: (b, h)),
            pl.BlockSpec((hd, seq), lambda b, h: (h, b)),
            pl.BlockSpec((seq, cum.shape[1]), lambda b, h: (b, 0)),
        ],
        out_specs=pl.BlockSpec((seq, hd), lambda b, h: (b, h)),
        out_shape=jax.ShapeDtypeStruct((t, n_heads * hd), BF16),
        scratch_shapes=[pltpu.VMEM((n_heads, seq, hd), BF16)],
        compiler_params=_params("arbitrary", "arbitrary"),
        name="fox_attention",
    )(q_t, k, v_t, cum)


def kernel(x, a_norm_w, a_in_proj, a_conv_w, a_conv_b, a_dt_bias, a_A_log, a_D, a_gnorm_w, a_out_proj, kv_norm_w, w_kvf, b_f, k_norm_w, b_norm_w, w_q, q_norm_w, w_o, ffn_norm_w, w_gate_up, w_down):
    bsz, seq, d_model = x.shape
    t = bsz * seq
    n_a = a_norm_w.shape[0]
    n_b = b_norm_w.shape[0]
    depth = n_a + n_b
    assert n_a >= 1, "the attention layers take the bf16 stream a Mamba-2 layer emits"
    h = x.reshape(t, d_model).astype(F32)
    hb = rinv = None
    w_out_b = a_out_proj.astype(BF16)
    w_down_b = w_down.astype(BF16)
    w_o_b = w_o.astype(BF16)

    for i in range(n_a):
        n_ssm_heads = a_dt_bias.shape[1]
        n_main = a_in_proj.shape[2] - n_ssm_heads
        w_main = a_in_proj[i, :, :n_main].astype(BF16)
        w_dt = jnp.pad(a_in_proj[i, :, n_main:], ((0, 0), (0, LANES - n_ssm_heads)))
        zx, dt_raw = _in_proj(h, a_norm_w[i], w_main, w_dt)
        y = _ssd(zx, dt_raw, a_conv_w[i], a_conv_b[i], a_dt_bias[i], a_A_log[i], a_D[i],
                 a_gnorm_w[i], bsz, seq)
        h, hb, rinv = _mm_res(y, w_out_b, i, h, True, name="out_proj")
        more = i + 1 < depth
        h, *norm = _ffn(h, hb, rinv, ffn_norm_w[i], w_gate_up, w_down_b, i, more)
        hb, rinv = norm if more else (None, None)

    n_att_heads = b_f.shape[0]
    d_att = n_att_heads * ATT_HEAD_DIM
    scale = ATT_HEAD_DIM ** -0.5 * LOG2E
    k = v_t = cum = None
    for j in range(n_b):
        if j == 0:
            w_f = jnp.pad(w_kvf[:, 2 * d_att:], ((0, 0), (0, LANES - n_att_heads)))
            b_pad = jnp.pad(b_f.astype(F32), (0, LANES - n_att_heads)).reshape(1, LANES)
            k, v_t, cum = _kv_proj(hb, rinv, kv_norm_w, w_kvf, w_f, b_pad, k_norm_w, d_att, seq)
        q_t = _q_proj(hb, rinv, b_norm_w[j], w_q, j, q_norm_w[j], scale)
        o = _attention(q_t, k, v_t, cum, bsz, seq, n_att_heads)
        h, hb, rinv = _mm_res(o, w_o_b, j, h, True, name="attn_out")
        more = n_a + j + 1 < depth
        h, *norm = _ffn(h, hb, rinv, ffn_norm_w[n_a + j], w_gate_up, w_down_b, n_a + j, more)
        hb, rinv = norm if more else (None, None)

    return h.reshape(bsz, seq, d_model).astype(x.dtype)
```

```python
import functools

import jax
import jax.numpy as jnp
from jax import lax
from jax.experimental import pallas as pl
from jax.experimental.pallas import tpu as pltpu

F32 = jnp.float32
BF16 = jnp.bfloat16
EPS = 1e-6

SSM_HEAD_DIM = 64
SSM_GROUPS = 8
D_STATE = 128
CONV_WIDTH = 4
SSD_CHUNK = 128
ATT_HEAD_DIM = 128

LANES = 128
SUBLANES = 8
BF16_SUBLANES = 16
VMEM_LIMIT_BYTES = 56 * 1024 * 1024

CONV_HALO = BF16_SUBLANES
LOG2E = 1.4426950408889634

NT_DIMS = (((1,), (1,)), ((), ()))
TN_DIMS = (((0,), (0,)), ((), ()))


def _params(*semantics):
    return pltpu.CompilerParams(dimension_semantics=semantics,
                                vmem_limit_bytes=VMEM_LIMIT_BYTES)


def _silu(v):
    half = 0.5 * v
    return half + half * jnp.tanh(half)


def _softplus(v):
    return jnp.maximum(v, 0.0) + jnp.log1p(jnp.exp(-jnp.abs(v)))


def _rms_rows(v, w):
    ms = jnp.mean(v * v, axis=-1, keepdims=True)
    return v * lax.rsqrt(ms + EPS) * w


SPLIT_PIECES = 3


def _bf16_pieces(v, n):
    parts, rest = [], v
    for _ in range(n):
        part = rest.astype(BF16)
        parts.append(part)
        rest = rest - part.astype(F32)
    return parts


def _scan_matrix(rows):
    tri = jnp.arange(rows)[None, :] <= jnp.arange(rows)[:, None]
    return jnp.concatenate([tri] * SPLIT_PIECES, axis=1).astype(BF16)


def _cumsum_rows(v, scan_ref):
    stacked = jnp.concatenate(_bf16_pieces(v, SPLIT_PIECES), axis=0)
    return jnp.dot(scan_ref[...], stacked, preferred_element_type=F32)


def _tile(n, pref):
    t = min(n, pref)
    assert n % t == 0, (n, pref)
    return t


def _inproj_kernel(x_ref, nw_ref, w_ref, wdt_ref, o_ref, dt_ref, xn_ref):
    @pl.when(pl.program_id(1) == 0)
    def _():
        xn = _rms_rows(x_ref[...], nw_ref[...]).astype(BF16)
        xn_ref[...] = xn
        dt_ref[...] = jnp.dot(xn, wdt_ref[...].astype(BF16), preferred_element_type=F32)

    o_ref[...] = jnp.dot(xn_ref[...], w_ref[...],
                         preferred_element_type=F32).astype(o_ref.dtype)


def _in_proj(h, norm_w, w_main, w_dt, tm=1024, tn=1024):
    t, d = h.shape
    n = w_main.shape[1]
    tm, tn = _tile(t, tm), _tile(n, tn)
    return pl.pallas_call(
        _inproj_kernel,
        grid=(t // tm, n // tn),
        in_specs=[
            pl.BlockSpec((tm, d), lambda i, j: (i, 0)),
            pl.BlockSpec((1, d), lambda i, j: (0, 0)),
            pl.BlockSpec((d, tn), lambda i, j: (0, j)),
            pl.BlockSpec((d, LANES), lambda i, j: (0, 0)),
        ],
        out_specs=[
            pl.BlockSpec((tm, tn), lambda i, j: (i, j)),
            pl.BlockSpec((tm, LANES), lambda i, j: (i, 0)),
        ],
        out_shape=[jax.ShapeDtypeStruct((t, n), BF16),
                   jax.ShapeDtypeStruct((t, LANES), F32)],
        scratch_shapes=[pltpu.VMEM((tm, d), BF16)],
        compiler_params=_params("parallel", "arbitrary"),
        name="in_proj",
    )(h, norm_w.reshape(1, d), w_main, w_dt)


def _ssd_kernel(z_ref, x_ref, bc_ref, dt_ref, cwx_ref, cwbc_ref, cbx_ref, cbbc_ref,
                dtb_ref, alog_ref, dskip_ref, gnw_ref, expand_ref, tri_ref, shift_ref,
                y_ref,
                state_ref, xext_ref, bcext_ref, bcs_ref):
    L = SSD_CHUNK
    P = SSM_HEAD_DIM
    N = D_STATE
    G = SSM_GROUPS
    d_inner = x_ref.shape[1]
    d_bc = bc_ref.shape[1]
    heads_per_group = d_inner // (G * P)
    gw = heads_per_group * P
    halo = CONV_HALO

    @pl.when(pl.program_id(1) == 0)
    def _():
        state_ref[...] = jnp.zeros_like(state_ref)
        xext_ref[0:halo, :] = jnp.zeros((halo, d_inner), BF16)
        bcext_ref[0:halo, :] = jnp.zeros((halo, d_bc), BF16)

    xext_ref[halo:halo + L, :] = x_ref[...]
    bcext_ref[halo:halo + L, :] = bc_ref[...]

    def conv_silu(ext_ref, cur_ref, w_ref, b_ref, cols):
        shifted = jnp.dot(shift_ref[...], ext_ref[:, cols], preferred_element_type=F32)
        acc = b_ref[:, cols] + w_ref[CONV_WIDTH - 1:CONV_WIDTH, cols] * cur_ref[:, cols].astype(F32)
        for k in range(CONV_WIDTH - 1):
            acc = acc + w_ref[k:k + 1, cols] * shifted[k * L:(k + 1) * L, :]
        return _silu(acc)

    for j in range(d_bc // gw):
        cols = slice(j * gw, (j + 1) * gw)
        bcs_ref[:, cols] = conv_silu(bcext_ref, bc_ref, cwbc_ref, cbbc_ref, cols).astype(BF16)
    bcext_ref[0:halo, :] = bcext_ref[L:L + halo, :]

    dtv = _softplus(dt_ref[...] + dtb_ref[...])
    a = dtv * (-LOG2E * jnp.exp(alog_ref[...]))
    acum = _cumsum_rows(a, tri_ref)
    a_last = acum[L - 1:L, :]
    acum_t = acum.T

    def split(v):
        return jnp.concatenate(_bf16_pieces(v, 2), axis=1)

    dt_hl = split(dtv)
    e_out_hl = split(jnp.exp2(acum))
    e_in_hl = split(dtv * jnp.exp2(a_last - acum))

    def expand(hl, cols):
        return jnp.dot(hl, expand_ref[:, cols], preferred_element_type=F32)

    row_id = lax.broadcasted_iota(jnp.int32, (L, L), 0)
    col_id = lax.broadcasted_iota(jnp.int32, (L, L), 1)
    causal = col_id <= row_id
    lane_id = lax.broadcasted_iota(jnp.int32, (L, 2 * P), 1)
    first_head = lane_id < P

    for g in range(G):
        gs = slice(g * gw, (g + 1) * gw)
        xs = conv_silu(xext_ref, x_ref, cwx_ref, cbx_ref, gs)
        xdt = (xs * expand(dt_hl, gs)).astype(BF16)
        xw = (xs * expand(e_in_hl, gs)).astype(BF16)
        e_out_x = expand(e_out_hl, gs)
        b_g = bcs_ref[:, g * N:(g + 1) * N]
        c_g = bcs_ref[:, G * N + g * N:G * N + (g + 1) * N]
        cb = lax.dot_general(c_g, b_g, NT_DIMS, preferred_element_type=F32)
        s_g = state_ref[:, gs]
        y_off = jnp.dot(c_g, s_g.astype(BF16), preferred_element_type=F32) * e_out_x
        upd = lax.dot_general(b_g, xw, TN_DIMS, preferred_element_type=F32)
        state_ref[:, gs] = s_g * e_out_x[L - 1:L, :] + upd
        y_pairs = []
        for pr in range(heads_per_group // 2):
            h0 = g * heads_per_group + 2 * pr
            ms = []
            for hh in (h0, h0 + 1):
                seg = acum[:, hh:hh + 1] - acum_t[hh:hh + 1, :]
                decay = jnp.exp2(jnp.where(causal, seg, -jnp.inf))
                ms.append((cb * decay).astype(BF16))
            m_cat = jnp.concatenate(ms, axis=1)
            xp = xdt[:, pr * 2 * P:(pr + 1) * 2 * P]
            zero = jnp.zeros_like(xp)
            rhs = jnp.concatenate([jnp.where(first_head, xp, zero),
                                   jnp.where(first_head, zero, xp)], axis=0)
            y_pairs.append(jnp.dot(m_cat, rhs, preferred_element_type=F32))
        y = jnp.concatenate(y_pairs, axis=1) + y_off + xs * dskip_ref[:, gs]
        y = y * _silu(z_ref[:, gs].astype(F32))
        y_ref[:, gs] = _rms_rows(y, gnw_ref[:, gs]).astype(y_ref.dtype)
    xext_ref[0:halo, :] = xext_ref[L:L + halo, :]


def _ssd(zx, dt_raw, conv_w, conv_b, dt_bias, a_log, d_skip, gnorm_w, bsz, seq):
    t = zx.shape[0]
    L, P, N, G = SSD_CHUNK, SSM_HEAD_DIM, D_STATE, SSM_GROUPS
    n_heads = dt_bias.shape[0]
    d_inner = n_heads * P
    d_bc = 2 * G * N
    assert zx.shape[1] == 2 * d_inner + d_bc and n_heads <= LANES
    assert d_inner % d_bc == 0 and seq % L == 0
    nc = seq // L
    bc_blk = 2 * d_inner // d_bc

    pad_h = LANES - n_heads
    dtb = jnp.pad(dt_bias.astype(F32), (0, pad_h)).reshape(1, LANES)
    alog = jnp.pad(a_log.astype(F32), (0, pad_h)).reshape(1, LANES)
    dskip_x = jnp.repeat(d_skip.astype(F32), P).reshape(1, d_inner)
    head_of_channel = jnp.arange(d_inner, dtype=jnp.int32) // P
    sel = (jnp.arange(LANES, dtype=jnp.int32)[:, None] == head_of_channel[None, :])
    expand_mat = jnp.concatenate([sel, sel], axis=0).astype(BF16)
    tri = _scan_matrix(L)
    n_shift = CONV_WIDTH - 1
    src_row = (CONV_HALO - n_shift + jnp.arange(n_shift)[:, None] + jnp.arange(L)[None, :]).reshape(-1)
    shift_mat = (src_row[:, None] == jnp.arange(CONV_HALO + L)[None, :]).astype(BF16)

    row = lambda b, c: b * nc + c
    const = lambda b, c: (0, 0)
    return pl.pallas_call(
        _ssd_kernel,
        grid=(bsz, nc),
        in_specs=[
            pl.BlockSpec((L, d_inner), lambda b, c: (row(b, c), 0)),
            pl.BlockSpec((L, d_inner), lambda b, c: (row(b, c), 1)),
            pl.BlockSpec((L, d_bc), lambda b, c: (row(b, c), bc_blk)),
            pl.BlockSpec((L, LANES), lambda b, c: (row(b, c), 0)),
            pl.BlockSpec((CONV_WIDTH, d_inner), const),
            pl.BlockSpec((CONV_WIDTH, d_bc), const),
            pl.BlockSpec((1, d_inner), const),
            pl.BlockSpec((1, d_bc), const),
            pl.BlockSpec((1, LANES), const),
            pl.BlockSpec((1, LANES), const),
            pl.BlockSpec((1, d_inner), const),
            pl.BlockSpec((1, d_inner), const),
            pl.BlockSpec((2 * LANES, d_inner), const),
            pl.BlockSpec(tri.shape, const),
            pl.BlockSpec((n_shift * L, CONV_HALO + L), const),
        ],
        out_specs=pl.BlockSpec((L, d_inner), lambda b, c: (row(b, c), 0)),
        out_shape=jax.ShapeDtypeStruct((t, d_inner), BF16),
        scratch_shapes=[
            pltpu.VMEM((N, d_inner), F32),
            pltpu.VMEM((CONV_HALO + L, d_inner), BF16),
            pltpu.VMEM((CONV_HALO + L, d_bc), BF16),
            pltpu.VMEM((L, d_bc), BF16),
        ],
        compiler_params=_params("parallel", "arbitrary"),
        name="ssd",
    )(zx, zx, zx, dt_raw,
      conv_w[:, :d_inner], conv_w[:, d_inner:],
      conv_b[:d_inner].reshape(1, d_inner), conv_b[d_inner:].reshape(1, d_bc),
      dtb, alog, dskip_x, gnorm_w.reshape(1, d_inner), expand_mat, tri, shift_mat)


def _mm_res_kernel(a_ref, w_ref, r_ref, h_ref, *norm_refs):
    h = r_ref[...] + jnp.dot(a_ref[...], w_ref[...], preferred_element_type=F32)
    h_ref[...] = h
    if norm_refs:
        hb_ref, rinv_ref = norm_refs
        hb_ref[...] = h.astype(BF16)
        ms = jnp.mean(h * h, axis=-1, keepdims=True)
        rinv_ref[...] = jnp.broadcast_to(lax.rsqrt(ms + EPS), rinv_ref.shape)


def _mm_res(a, w_layers, layer, res, emit_norm, tm=256, name="mm_res"):
    t, k = a.shape
    n = w_layers.shape[2]
    tm = _tile(t, tm)
    out_specs = [pl.BlockSpec((tm, n), lambda i: (i, 0))]
    out_shape = [jax.ShapeDtypeStruct((t, n), F32)]
    if emit_norm:
        out_specs += [pl.BlockSpec((tm, n), lambda i: (i, 0)),
                      pl.BlockSpec((tm, LANES), lambda i: (i, 0))]
        out_shape += [jax.ShapeDtypeStruct((t, n), BF16),
                      jax.ShapeDtypeStruct((t, LANES), F32)]
    return pl.pallas_call(
        _mm_res_kernel,
        grid=(t // tm,),
        in_specs=[
            pl.BlockSpec((tm, k), lambda i: (i, 0)),
            pl.BlockSpec((None, k, n), lambda i: (layer, 0, 0), pipeline_mode=pl.Buffered(1)),
            pl.BlockSpec((tm, n), lambda i: (i, 0)),
        ],
        out_specs=out_specs,
        out_shape=out_shape,
        compiler_params=_params("arbitrary"),
        name=name,
    )(a, w_layers, res)


def _stage_weight(w_ref, nw_ref, wb_ref):
    wb_ref[...] = (w_ref[...] * nw_ref[...]).astype(BF16)


def _scale_rows(acc, rinv):
    return jnp.concatenate([acc[:, c * LANES:(c + 1) * LANES] * rinv
                            for c in range(acc.shape[1] // LANES)], axis=1)


ROW_PARTS = 2


def _projected_parts(hb_ref, rinv_ref, wb_ref):
    rs = hb_ref.shape[0] // ROW_PARTS
    parts = [slice(p * rs, (p + 1) * rs) for p in range(ROW_PARTS)]
    accs = [jnp.dot(hb_ref[rows, :], wb_ref[...], preferred_element_type=F32) for rows in parts]
    return [(rows, _scale_rows(acc, rinv_ref[rows, :])) for rows, acc in zip(parts, accs)]


def _ffn_up_kernel(hb_ref, rinv_ref, nw_ref, wg_ref, wu_ref, o_ref, wgb_ref, wub_ref):
    @pl.when(pl.program_id(1) == 0)
    def _():
        _stage_weight(wg_ref, nw_ref, wgb_ref)
        _stage_weight(wu_ref, nw_ref, wub_ref)

    hb = hb_ref[...]
    rinv = rinv_ref[...]
    g = _scale_rows(jnp.dot(hb, wgb_ref[...], preferred_element_type=F32), rinv)
    u = _scale_rows(jnp.dot(hb, wub_ref[...], preferred_element_type=F32), rinv)
    o_ref[...] = (_silu(g) * u).astype(o_ref.dtype)


def _ffn_up(hb, rinv, norm_w, w_gate_up, layer, tm=1024, tn=512):
    t, d = hb.shape
    d_ff = w_gate_up.shape[2] // 2
    tm, tn = _tile(t, tm), _tile(d_ff, tn)
    nj = d_ff // tn
    return pl.pallas_call(
        _ffn_up_kernel,
        grid=(nj, t // tm),
        in_specs=[
            pl.BlockSpec((tm, d), lambda j, i: (i, 0)),
            pl.BlockSpec((tm, LANES), lambda j, i: (i, 0)),
            pl.BlockSpec((d, 1), lambda j, i: (0, 0)),
            pl.BlockSpec((None, d, tn), lambda j, i: (layer, 0, j)),
            pl.BlockSpec((None, d, tn), lambda j, i: (layer, 0, j + nj)),
        ],
        out_specs=pl.BlockSpec((tm, tn), lambda j, i: (i, j)),
        out_shape=jax.ShapeDtypeStruct((t, d_ff), BF16),
        scratch_shapes=[pltpu.VMEM((d, tn), BF16), pltpu.VMEM((d, tn), BF16)],
        compiler_params=_params("arbitrary", "arbitrary"),
        name="ffn_up",
    )(hb, rinv, norm_w.reshape(d, 1), w_gate_up, w_gate_up)


def _ffn(h, hb, rinv, norm_w, w_gate_up, w_down_b, layer, emit_norm):
    act = _ffn_up(hb, rinv, norm_w, w_gate_up, layer)
    return _mm_res(act, w_down_b, layer, h, emit_norm, name="ffn_down")


def _head_norm(acc, hw, scale):
    parts = []
    for c in range(acc.shape[1] // ATT_HEAD_DIM):
        part = acc[:, c * ATT_HEAD_DIM:(c + 1) * ATT_HEAD_DIM]
        parts.append(_rms_rows(part, hw) * scale)
    return jnp.concatenate(parts, axis=1)


def _qproj_kernel(hb_ref, rinv_ref, nw_ref, w_ref, hw_ref, o_ref, wb_ref, *, scale):
    @pl.when(pl.program_id(1) == 0)
    def _():
        _stage_weight(w_ref, nw_ref, wb_ref)

    for rows, acc in _projected_parts(hb_ref, rinv_ref, wb_ref):
        o_ref[:, rows] = _head_norm(acc, hw_ref[...], scale).T.astype(o_ref.dtype)


def _q_proj(hb, rinv, norm_w, w_layers, layer, head_w, scale, tm=1024, tn=512):
    t, d = hb.shape
    n = w_layers.shape[2]
    tm, tn = _tile(t, tm), _tile(n, tn)
    return pl.pallas_call(
        functools.partial(_qproj_kernel, scale=scale),
        grid=(n // tn, t // tm),
        in_specs=[
            pl.BlockSpec((tm, d), lambda j, i: (i, 0)),
            pl.BlockSpec((tm, LANES), lambda j, i: (i, 0)),
            pl.BlockSpec((d, 1), lambda j, i: (0, 0)),
            pl.BlockSpec((None, d, tn), lambda j, i: (layer, 0, j)),
            pl.BlockSpec((1, ATT_HEAD_DIM), lambda j, i: (0, 0)),
        ],
        out_specs=pl.BlockSpec((tn, tm), lambda j, i: (j, i)),
        out_shape=jax.ShapeDtypeStruct((n, t), BF16),
        scratch_shapes=[pltpu.VMEM((d, tn), BF16)],
        compiler_params=_params("arbitrary", "arbitrary"),
        name="q_proj",
    )(hb, rinv, norm_w.reshape(d, 1), w_layers, head_w.reshape(1, ATT_HEAD_DIM))


def _kvproj_kernel(hb_ref, rinv_ref, nw_ref, w_ref, wf_ref, bf_ref, hw_ref, tri_ref,
                   k_ref, vt_ref, cum_ref, wb_ref, wfb_ref, carry_ref, *, n_k_blocks, blocks_per_seq):
    j = pl.program_id(0)
    i = pl.program_id(1)
    tm = hb_ref.shape[0]

    @pl.when(i == 0)
    def _():
        _stage_weight(w_ref, nw_ref, wb_ref)

    @pl.when(j == 0)
    def _():
        @pl.when(i == 0)
        def _():
            _stage_weight(wf_ref, nw_ref, wfb_ref)

        logit = jnp.dot(hb_ref[...], wfb_ref[...], preferred_element_type=F32) * rinv_ref[...]
        log_f = -_softplus(-(logit + bf_ref[...]))

        @pl.when(i % blocks_per_seq == 0)
        def _():
            carry_ref[...] = jnp.zeros_like(carry_ref)

        rows_per_scan = tri_ref.shape[0]
        carry = carry_ref[0:1, :]
        for sb in range(tm // rows_per_scan):
            rows = slice(sb * rows_per_scan, (sb + 1) * rows_per_scan)
            cum = _cumsum_rows(log_f[rows, :], tri_ref) + carry
            cum_ref[rows, :] = jnp.concatenate(_bf16_pieces(-LOG2E * cum, BIAS_PIECES), axis=1)
            carry = cum[rows_per_scan - 1:rows_per_scan, :]
        carry_ref[...] = jnp.broadcast_to(carry, carry_ref.shape)

    @pl.when(j < n_k_blocks)
    def _():
        for rows, acc in _projected_parts(hb_ref, rinv_ref, wb_ref):
            k_ref[rows, :] = _head_norm(acc, hw_ref[...], 1.0).astype(k_ref.dtype)

    @pl.when(j >= n_k_blocks)
    def _():
        for rows, acc in _projected_parts(hb_ref, rinv_ref, wb_ref):
            vt_ref[:, rows] = acc.T.astype(vt_ref.dtype)


def _kv_proj(hb, rinv, norm_w, w_kvf, w_f, b_f, head_w, d_att, seq, tm=1024, tn=512, scan_rows=256):
    t, d = hb.shape
    n = 2 * d_att
    tm, tn = _tile(seq, tm), _tile(d_att, tn)
    ni = t // tm
    nk = d_att // tn
    tri = _scan_matrix(_tile(tm, scan_rows))
    kern = functools.partial(_kvproj_kernel, n_k_blocks=nk, blocks_per_seq=seq // tm)
    return pl.pallas_call(
        kern,
        grid=(n // tn, ni),
        in_specs=[
            pl.BlockSpec((tm, d), lambda j, i: (i, 0)),
            pl.BlockSpec((tm, LANES), lambda j, i: (i, 0)),
            pl.BlockSpec((d, 1), lambda j, i: (0, 0)),
            pl.BlockSpec((d, tn), lambda j, i: (0, j)),
            pl.BlockSpec((d, LANES), lambda j, i: (0, 0)),
            pl.BlockSpec((1, LANES), lambda j, i: (0, 0)),
            pl.BlockSpec((1, ATT_HEAD_DIM), lambda j, i: (0, 0)),
            pl.BlockSpec(tri.shape, lambda j, i: (0, 0)),
        ],
        out_specs=[
            pl.BlockSpec((tm, tn), lambda j, i: (jnp.where(j < nk, i, ni - 1), jnp.minimum(j, nk - 1))),
            pl.BlockSpec((tn, tm), lambda j, i: (jnp.maximum(j - nk, 0), jnp.where(j < nk, 0, i))),
            pl.BlockSpec((tm, BIAS_PIECES * LANES), lambda j, i: (jnp.where(j == 0, i, ni - 1), 0)),
        ],
        out_shape=[jax.ShapeDtypeStruct((t, d_att), BF16),
                   jax.ShapeDtypeStruct((d_att, t), BF16),
                   jax.ShapeDtypeStruct((t, BIAS_PIECES * LANES), BF16)],
        scratch_shapes=[pltpu.VMEM((d, tn), BF16), pltpu.VMEM((d, LANES), BF16),
                        pltpu.VMEM((SUBLANES, LANES), F32)],
        compiler_params=_params("arbitrary", "arbitrary"),
        name="kv_proj",
    )(hb, rinv, norm_w.reshape(d, 1), w_kvf, w_f, b_f, head_w.reshape(1, ATT_HEAD_DIM), tri)


BIAS_PIECES = SPLIT_PIECES


def _attn_kernel(qt_ref, k_ref, vt_ref, cum_ref, o_ref, kb_ref, *, tq, unroll):
    seq, hd = k_ref.shape
    nq = seq // tq
    head = pl.program_id(1)
    sub = lax.broadcasted_iota(jnp.int32, (hd, 1), 0)
    ones_rows = jnp.broadcast_to(jnp.where(sub < BIAS_PIECES, 1.0, 0.0), (hd, tq)).astype(BF16)

    @pl.when(head == 0)
    def _():
        n_heads = kb_ref.shape[0]
        src = lax.broadcasted_iota(jnp.int32, (cum_ref.shape[1], n_heads * hd), 0)
        dst = lax.broadcasted_iota(jnp.int32, (cum_ref.shape[1], n_heads * hd), 1)
        dst_head, dst_lane = dst // hd, dst % hd
        pick = jnp.where((src == dst_lane * LANES + dst_head) & (dst_lane < BIAS_PIECES),
                         1.0, 0.0).astype(BF16)

        def prepare_rows(r, carry):
            rows = pl.ds(pl.multiple_of(r * tq, tq), tq)
            placed = jnp.dot(cum_ref[rows, :], pick, preferred_element_type=F32).astype(BF16)
            for hh in range(n_heads):
                kb_ref[hh, rows, :] = placed[:, hh * hd:(hh + 1) * hd]
            return carry

        lax.fori_loop(0, nq, prepare_rows, 0)

    tile_row = lax.broadcasted_iota(jnp.int32, (BF16_SUBLANES, 1), 0)
    denom_rows = jnp.broadcast_to(jnp.where(tile_row == 0, 1.0, 0.0), (BF16_SUBLANES, tq)).astype(BF16)
    key_id = lax.broadcasted_iota(jnp.int32, (tq, tq), 0)
    qry_id = lax.broadcasted_iota(jnp.int32, (tq, tq), 1)
    causal = key_id <= qry_id

    def scores(qt, start):
        keys = pl.ds(start, tq)
        k_aug = jnp.concatenate([k_ref[keys, :], kb_ref[head, keys, :]], axis=1)
        return jnp.dot(k_aug, qt, preferred_element_type=F32)

    def update(s, start, carry, diagonal):
        m, acc = carry
        if diagonal:
            s = jnp.where(causal, s, -jnp.inf)
        m_new = jnp.maximum(m, jnp.max(s, axis=0, keepdims=True))
        alpha = jnp.exp2(m - m_new)
        p = jnp.exp2(s - m_new).astype(BF16)
        vt_aug = jnp.concatenate([vt_ref[:, pl.ds(start, tq)], denom_rows], axis=0)
        acc = alpha * acc + jnp.dot(vt_aug, p, preferred_element_type=F32)
        return m_new, acc

    def blocks(qt, starts, carry, last_is_diagonal):
        s_next = scores(qt, starts[0])
        for u, start in enumerate(starts):
            s_cur = s_next
            if u + 1 < len(starts):
                s_next = scores(qt, starts[u + 1])
            carry = update(s_cur, start, carry, last_is_diagonal and u + 1 == len(starts))
        return carry

    for qi in range(nq):
        cols = slice(qi * tq, (qi + 1) * tq)
        qt = jnp.concatenate([qt_ref[:, cols], ones_rows], axis=0)
        carry = (jnp.full((1, tq), -jnp.inf, F32), jnp.zeros((hd + BF16_SUBLANES, tq), F32))
        n_loop = qi // unroll
        if n_loop > 0:
            def group(ki, c, qt=qt):
                base = pl.multiple_of(ki * unroll * tq, unroll * tq)
                return blocks(qt, [base + u * tq for u in range(unroll)], c, False)
            carry = lax.fori_loop(0, n_loop, group, carry)
        tail_starts = [kb * tq for kb in range(n_loop * unroll, qi + 1)]
        _, acc = blocks(qt, tail_starts, carry, True)
        o_ref[cols, :] = (acc[0:hd, :] / acc[hd:hd + 1, :]).T.astype(o_ref.dtype)


def _attention(q_t, k, v_t, cum, bsz, seq, n_heads, tq=512, unroll=4):
    t = k.shape[0]
    tq = _tile(seq, tq)
    hd = ATT_HEAD_DIM
    return pl.pallas_call(
        functools.partial(_attn_kernel, tq=tq, unroll=unroll),
        grid=(bsz, n_heads),
        in_specs=[
            pl.BlockSpec((hd, seq), lambda b, h: (h, b)),
            pl.BlockSpec((seq, hd), lambda b, h: (b, h)),
            pl.BlockSpec((hd, seq), lambda b, h: (h, b)),
            pl.BlockSpec((seq, cum.shape[1]), lambda b, h: (b, 0)),
        ],
        out_specs=pl.BlockSpec((seq, hd), lambda b, h: (b, h)),
        out_shape=jax.ShapeDtypeStruct((t, n_heads * hd), BF16),
        scratch_shapes=[pltpu.VMEM((n_heads, seq, hd), BF16)],
        compiler_params=_params("arbitrary", "arbitrary"),
        name="fox_attention",
    )(q_t, k, v_t, cum)


def kernel(x, a_norm_w, a_in_proj, a_conv_w, a_conv_b, a_dt_bias, a_A_log, a_D, a_gnorm_w, a_out_proj, kv_norm_w, w_kvf, b_f, k_norm_w, b_norm_w, w_q, q_norm_w, w_o, ffn_norm_w, w_gate_up, w_down):
    bsz, seq, d_model = x.shape
    t = bsz * seq
    n_a = a_norm_w.shape[0]
    n_b = b_norm_w.shape[0]
    depth = n_a + n_b
    assert n_a >= 1, "the attention layers take the bf16 stream a Mamba-2 layer emits"
    h = x.reshape(t, d_model).astype(F32)
    hb = rinv = None
    w_out_b = a_out_proj.astype(BF16)
    w_down_b = w_down.astype(BF16)
    w_o_b = w_o.astype(BF16)

    for i in range(n_a):
        n_ssm_heads = a_dt_bias.shape[1]
        n_main = a_in_proj.shape[2] - n_ssm_heads
        w_main = a_in_proj[i, :, :n_main].astype(BF16)
        w_dt = jnp.pad(a_in_proj[i, :, n_main:], ((0, 0), (0, LANES - n_ssm_heads)))
        zx, dt_raw = _in_proj(h, a_norm_w[i], w_main, w_dt)
        y = _ssd(zx, dt_raw, a_conv_w[i], a_conv_b[i], a_dt_bias[i], a_A_log[i], a_D[i],
                 a_gnorm_w[i], bsz, seq)
        h, hb, rinv = _mm_res(y, w_out_b, i, h, True, name="out_proj")
        more = i + 1 < depth
        h, *norm = _ffn(h, hb, rinv, ffn_norm_w[i], w_gate_up, w_down_b, i, more)
        hb, rinv = norm if more else (None, None)

    n_att_heads = b_f.shape[0]
    d_att = n_att_heads * ATT_HEAD_DIM
    scale = ATT_HEAD_DIM ** -0.5 * LOG2E
    k = v_t = cum = None
    for j in range(n_b):
        if j == 0:
            w_f = jnp.pad(w_kvf[:, 2 * d_att:], ((0, 0), (0, LANES - n_att_heads)))
            b_pad = jnp.pad(b_f.astype(F32), (0, LANES - n_att_heads)).reshape(1, LANES)
            k, v_t, cum = _kv_proj(hb, rinv, kv_norm_w, w_kvf, w_f, b_pad, k_norm_w, d_att, seq)
        q_t = _q_proj(hb, rinv, b_norm_w[j], w_q, j, q_norm_w[j], scale)
        o = _attention(q_t, k, v_t, cum, bsz, seq, n_att_heads)
        h, hb, rinv = _mm_res(o, w_o_b, j, h, True, name="attn_out")
        more = n_a + j + 1 < depth
        h, *norm = _ffn(h, hb, rinv, ffn_norm_w[n_a + j], w_gate_up, w_down_b, n_a + j, more)
        hb, rinv = norm if more else (None, None)

    return h.reshape(bsz, seq, d_model).astype(x.dtype)
```

```python
import functools

import jax
import jax.numpy as jnp
from jax import lax
from jax.experimental import pallas as pl
from jax.experimental.pallas import tpu as pltpu

F32 = jnp.float32
BF16 = jnp.bfloat16
EPS = 1e-6

SSM_HEAD_DIM = 64
SSM_GROUPS = 8
D_STATE = 128
CONV_WIDTH = 4
SSD_CHUNK = 128
ATT_HEAD_DIM = 128

LANES = 128
SUBLANES = 8
BF16_SUBLANES = 16
VMEM_LIMIT_BYTES = 56 * 1024 * 1024

CONV_HALO = BF16_SUBLANES
LOG2E = 1.4426950408889634

NT_DIMS = (((1,), (1,)), ((), ()))
TN_DIMS = (((0,), (0,)), ((), ()))


def _params(*semantics):
    return pltpu.CompilerParams(dimension_semantics=semantics,
                                vmem_limit_bytes=VMEM_LIMIT_BYTES)


def _silu(v):
    half = 0.5 * v
    return half + half * jnp.tanh(half)


def _softplus(v):
    return jnp.maximum(v, 0.0) + jnp.log1p(jnp.exp(-jnp.abs(v)))


def _rms_rows(v, w):
    ms = jnp.mean(v * v, axis=-1, keepdims=True)
    return v * lax.rsqrt(ms + EPS) * w


SPLIT_PIECES = 3


def _bf16_pieces(v, n):
    parts, rest = [], v
    for _ in range(n):
        part = rest.astype(BF16)
        parts.append(part)
        rest = rest - part.astype(F32)
    return parts


def _scan_matrix(rows):
    tri = jnp.arange(rows)[None, :] <= jnp.arange(rows)[:, None]
    return jnp.concatenate([tri] * SPLIT_PIECES, axis=1).astype(BF16)


def _cumsum_rows(v, scan_ref):
    stacked = jnp.concatenate(_bf16_pieces(v, SPLIT_PIECES), axis=0)
    return jnp.dot(scan_ref[...], stacked, preferred_element_type=F32)


def _tile(n, pref):
    t = min(n, pref)
    assert n % t == 0, (n, pref)
    return t


def _inproj_kernel(x_ref, nw_ref, w_ref, wdt_ref, o_ref, dt_ref, xn_ref):
    @pl.when(pl.program_id(1) == 0)
    def _():
        xn = _rms_rows(x_ref[...], nw_ref[...]).astype(BF16)
        xn_ref[...] = xn
        dt_ref[...] = jnp.dot(xn, wdt_ref[...].astype(BF16), preferred_element_type=F32)

    o_ref[...] = jnp.dot(xn_ref[...], w_ref[...],
                         preferred_element_type=F32).astype(o_ref.dtype)


def _in_proj(h, norm_w, w_main, w_dt, tm=1024, tn=1024):
    t, d = h.shape
    n = w_main.shape[1]
    tm, tn = _tile(t, tm), _tile(n, tn)
    return pl.pallas_call(
        _inproj_kernel,
        grid=(t // tm, n // tn),
        in_specs=[
            pl.BlockSpec((tm, d), lambda i, j: (i, 0)),
            pl.BlockSpec((1, d), lambda i, j: (0, 0)),
            pl.BlockSpec((d, tn), lambda i, j: (0, j)),
            pl.BlockSpec((d, LANES), lambda i, j: (0, 0)),
        ],
        out_specs=[
            pl.BlockSpec((tm, tn), lambda i, j: (i, j)),
            pl.BlockSpec((tm, LANES), lambda i, j: (i, 0)),
        ],
        out_shape=[jax.ShapeDtypeStruct((t, n), BF16),
                   jax.ShapeDtypeStruct((t, LANES), F32)],
        scratch_shapes=[pltpu.VMEM((tm, d), BF16)],
        compiler_params=_params("parallel", "arbitrary"),
        name="in_proj",
    )(h, norm_w.reshape(1, d), w_main, w_dt)


def _ssd_kernel(z_ref, x_ref, bc_ref, dt_ref, cwx_ref, cwbc_ref, cbx_ref, cbbc_ref,
                dtb_ref, alog_ref, dskip_ref, gnw_ref, expand_ref, tri_ref, shift_ref,
                y_ref,
                state_ref, xext_ref, bcext_ref, bcs_ref):
    L = SSD_CHUNK
    P = SSM_HEAD_DIM
    N = D_STATE
    G = SSM_GROUPS
    d_inner = x_ref.shape[1]
    d_bc = bc_ref.shape[1]
    heads_per_group = d_inner // (G * P)
    gw = heads_per_group * P
    halo = CONV_HALO

    @pl.when(pl.program_id(1) == 0)
    def _():
        state_ref[...] = jnp.zeros_like(state_ref)
        xext_ref[0:halo, :] = jnp.zeros((halo, d_inner), BF16)
        bcext_ref[0:halo, :] = jnp.zeros((halo, d_bc), BF16)

    xext_ref[halo:halo + L, :] = x_ref[...]
    bcext_ref[halo:halo + L, :] = bc_ref[...]

    def conv_silu(ext_ref, cur_ref, w_ref, b_ref, cols):
        shifted = jnp.dot(shift_ref[...], ext_ref[:, cols], preferred_element_type=F32)
        last = CONV_WIDTH - 1
        acc = 0.5 * b_ref[:, cols] + (0.5 * w_ref[last:last + 1, cols]) * cur_ref[:, cols].astype(F32)
        for k in range(last):
            acc = acc + (0.5 * w_ref[k:k + 1, cols]) * shifted[k * L:(k + 1) * L, :]
        return acc + acc * jnp.tanh(acc)

    for j in range(d_bc // gw):
        cols = slice(j * gw, (j + 1) * gw)
        bcs_ref[:, cols] = conv_silu(bcext_ref, bc_ref, cwbc_ref, cbbc_ref, cols).astype(BF16)
    bcext_ref[0:halo, :] = bcext_ref[L:L + halo, :]

    dtv = _softplus(dt_ref[...] + dtb_ref[...])
    a = dtv * (-LOG2E * jnp.exp(alog_ref[...]))
    acum = _cumsum_rows(a, tri_ref)
    a_last = acum[L - 1:L, :]
    acum_t = acum.T

    def split(v):
        return jnp.concatenate(_bf16_pieces(v, 2), axis=1)

    dt_hl = split(dtv)
    e_out_hl = split(jnp.exp2(acum))
    e_in_hl = split(dtv * jnp.exp2(a_last - acum))

    def expand(hl, cols):
        return jnp.dot(hl, expand_ref[:, cols], preferred_element_type=F32)

    row_id = lax.broadcasted_iota(jnp.int32, (L, L), 0)
    col_id = lax.broadcasted_iota(jnp.int32, (L, L), 1)
    causal = col_id <= row_id
    lane_id = lax.broadcasted_iota(jnp.int32, (L, 2 * P), 1)
    first_head = lane_id < P

    for g in range(G):
        gs = slice(g * gw, (g + 1) * gw)
        xs = conv_silu(xext_ref, x_ref, cwx_ref, cbx_ref, gs)
        xdt = (xs * expand(dt_hl, gs)).astype(BF16)
        xw = (xs * expand(e_in_hl, gs)).astype(BF16)
        e_out_x = expand(e_out_hl, gs)
        b_g = bcs_ref[:, g * N:(g + 1) * N]
        c_g = bcs_ref[:, G * N + g * N:G * N + (g + 1) * N]
        cb = lax.dot_general(c_g, b_g, NT_DIMS, preferred_element_type=F32)
        s_g = state_ref[:, gs]
        y_off = jnp.dot(c_g, s_g.astype(BF16), preferred_element_type=F32) * e_out_x
        upd = lax.dot_general(b_g, xw, TN_DIMS, preferred_element_type=F32)
        state_ref[:, gs] = s_g * e_out_x[L - 1:L, :] + upd
        y_pairs = []
        for pr in range(heads_per_group // 2):
            h0 = g * heads_per_group + 2 * pr
            ms = []
            for hh in (h0, h0 + 1):
                seg = acum[:, hh:hh + 1] - acum_t[hh:hh + 1, :]
                decay = jnp.exp2(jnp.where(causal, seg, -jnp.inf))
                ms.append((cb * decay).astype(BF16))
            m_cat = jnp.concatenate(ms, axis=1)
            xp = xdt[:, pr * 2 * P:(pr + 1) * 2 * P]
            zero = jnp.zeros_like(xp)
            rhs = jnp.concatenate([jnp.where(first_head, xp, zero),
                                   jnp.where(first_head, zero, xp)], axis=0)
            y_pairs.append(jnp.dot(m_cat, rhs, preferred_element_type=F32))
        y = jnp.concatenate(y_pairs, axis=1) + y_off + xs * dskip_ref[:, gs]
        y = y * _silu(z_ref[:, gs].astype(F32))
        y_ref[:, gs] = _rms_rows(y, gnw_ref[:, gs]).astype(y_ref.dtype)
    xext_ref[0:halo, :] = xext_ref[L:L + halo, :]


def _ssd(zx, dt_raw, conv_w, conv_b, dt_bias, a_log, d_skip, gnorm_w, bsz, seq):
    t = zx.shape[0]
    L, P, N, G = SSD_CHUNK, SSM_HEAD_DIM, D_STATE, SSM_GROUPS
    n_heads = dt_bias.shape[0]
    d_inner = n_heads * P
    d_bc = 2 * G * N
    assert zx.shape[1] == 2 * d_inner + d_bc and n_heads <= LANES
    assert d_inner % d_bc == 0 and seq % L == 0
    nc = seq // L
    bc_blk = 2 * d_inner // d_bc

    pad_h = LANES - n_heads
    dtb = jnp.pad(dt_bias.astype(F32), (0, pad_h)).reshape(1, LANES)
    alog = jnp.pad(a_log.astype(F32), (0, pad_h)).reshape(1, LANES)
    dskip_x = jnp.repeat(d_skip.astype(F32), P).reshape(1, d_inner)
    head_of_channel = jnp.arange(d_inner, dtype=jnp.int32) // P
    sel = (jnp.arange(LANES, dtype=jnp.int32)[:, None] == head_of_channel[None, :])
    expand_mat = jnp.concatenate([sel, sel], axis=0).astype(BF16)
    tri = _scan_matrix(L)
    n_shift = CONV_WIDTH - 1
    src_row = (CONV_HALO - n_shift + jnp.arange(n_shift)[:, None] + jnp.arange(L)[None, :]).reshape(-1)
    shift_mat = (src_row[:, None] == jnp.arange(CONV_HALO + L)[None, :]).astype(BF16)

    row = lambda b, c: b * nc + c
    const = lambda b, c: (0, 0)
    return pl.pallas_call(
        _ssd_kernel,
        grid=(bsz, nc),
        in_specs=[
            pl.BlockSpec((L, d_inner), lambda b, c: (row(b, c), 0)),
            pl.BlockSpec((L, d_inner), lambda b, c: (row(b, c), 1)),
            pl.BlockSpec((L, d_bc), lambda b, c: (row(b, c), bc_blk)),
            pl.BlockSpec((L, LANES), lambda b, c: (row(b, c), 0)),
            pl.BlockSpec((CONV_WIDTH, d_inner), const),
            pl.BlockSpec((CONV_WIDTH, d_bc), const),
            pl.BlockSpec((1, d_inner), const),
            pl.BlockSpec((1, d_bc), const),
            pl.BlockSpec((1, LANES), const),
            pl.BlockSpec((1, LANES), const),
            pl.BlockSpec((1, d_inner), const),
            pl.BlockSpec((1, d_inner), const),
            pl.BlockSpec((2 * LANES, d_inner), const),
            pl.BlockSpec(tri.shape, const),
            pl.BlockSpec((n_shift * L, CONV_HALO + L), const),
        ],
        out_specs=pl.BlockSpec((L, d_inner), lambda b, c: (row(b, c), 0)),
        out_shape=jax.ShapeDtypeStruct((t, d_inner), BF16),
        scratch_shapes=[
            pltpu.VMEM((N, d_inner), F32),
            pltpu.VMEM((CONV_HALO + L, d_inner), BF16),
            pltpu.VMEM((CONV_HALO + L, d_bc), BF16),
            pltpu.VMEM((L, d_bc), BF16),
        ],
        compiler_params=_params("parallel", "arbitrary"),
        name="ssd",
    )(zx, zx, zx, dt_raw,
      conv_w[:, :d_inner], conv_w[:, d_inner:],
      conv_b[:d_inner].reshape(1, d_inner), conv_b[d_inner:].reshape(1, d_bc),
      dtb, alog, dskip_x, gnorm_w.reshape(1, d_inner), expand_mat, tri, shift_mat)


def _mm_res_kernel(a_ref, w_ref, r_ref, h_ref, *norm_refs):
    h = r_ref[...] + jnp.dot(a_ref[...], w_ref[...], preferred_element_type=F32)
    h_ref[...] = h
    if norm_refs:
        hb_ref, rinv_ref = norm_refs
        hb_ref[...] = h.astype(BF16)
        ms = jnp.mean(h * h, axis=-1, keepdims=True)
        rinv_ref[...] = jnp.broadcast_to(lax.rsqrt(ms + EPS), rinv_ref.shape)


MM_RES_ROW_CHOICES = (512, 256)
MM_RES_VMEM_BUDGET = 50 * 1024 * 1024


def _mm_res_rows(t, k, n, emit_norm):
    for tm in MM_RES_ROW_CHOICES:
        tiles = tm * k * 2 + 2 * tm * n * 4 + (tm * n * 2 + tm * LANES * 4 if emit_norm else 0)
        if t % tm == 0 and k * n * 2 + 2 * tiles <= MM_RES_VMEM_BUDGET:
            return tm
    return _tile(t, MM_RES_ROW_CHOICES[-1])


def _mm_res(a, w_layers, layer, res, emit_norm, name="mm_res"):
    t, k = a.shape
    n = w_layers.shape[2]
    tm = _mm_res_rows(t, k, n, emit_norm)
    out_specs = [pl.BlockSpec((tm, n), lambda i: (i, 0))]
    out_shape = [jax.ShapeDtypeStruct((t, n), F32)]
    if emit_norm:
        out_specs += [pl.BlockSpec((tm, n), lambda i: (i, 0)),
                      pl.BlockSpec((tm, LANES), lambda i: (i, 0))]
        out_shape += [jax.ShapeDtypeStruct((t, n), BF16),
                      jax.ShapeDtypeStruct((t, LANES), F32)]
    return pl.pallas_call(
        _mm_res_kernel,
        grid=(t // tm,),
        in_specs=[
            pl.BlockSpec((tm, k), lambda i: (i, 0)),
            pl.BlockSpec((None, k, n), lambda i: (layer, 0, 0), pipeline_mode=pl.Buffered(1)),
            pl.BlockSpec((tm, n), lambda i: (i, 0)),
        ],
        out_specs=out_specs,
        out_shape=out_shape,
        compiler_params=_params("arbitrary"),
        name=name,
    )(a, w_layers, res)


def _stage_weight(w_ref, nw_ref, wb_ref):
    wb_ref[...] = (w_ref[...] * nw_ref[...]).astype(BF16)


def _scale_rows(acc, rinv):
    return jnp.concatenate([acc[:, c * LANES:(c + 1) * LANES] * rinv
                            for c in range(acc.shape[1] // LANES)], axis=1)


ROW_PARTS = 2


def _projected_parts(hb_ref, rinv_ref, wb_ref):
    rs = hb_ref.shape[0] // ROW_PARTS
    parts = [slice(p * rs, (p + 1) * rs) for p in range(ROW_PARTS)]
    accs = [jnp.dot(hb_ref[rows, :], wb_ref[...], preferred_element_type=F32) for rows in parts]
    return [(rows, _scale_rows(acc, rinv_ref[rows, :])) for rows, acc in zip(parts, accs)]


def _ffn_up_kernel(hb_ref, rinv_ref, nw_ref, wg_ref, wu_ref, o_ref, wgb_ref, wub_ref):
    @pl.when(pl.program_id(1) == 0)
    def _():
        _stage_weight(wg_ref, nw_ref, wgb_ref)
        _stage_weight(wu_ref, nw_ref, wub_ref)

    hb = hb_ref[...]
    rinv = rinv_ref[...]
    g = _scale_rows(jnp.dot(hb, wgb_ref[...], preferred_element_type=F32), rinv)
    u = _scale_rows(jnp.dot(hb, wub_ref[...], preferred_element_type=F32), rinv)
    o_ref[...] = (_silu(g) * u).astype(o_ref.dtype)


def _ffn_up(hb, rinv, norm_w, w_gate_up, layer, tm=1024, tn=512):
    t, d = hb.shape
    d_ff = w_gate_up.shape[2] // 2
    tm, tn = _tile(t, tm), _tile(d_ff, tn)
    nj = d_ff // tn
    return pl.pallas_call(
        _ffn_up_kernel,
        grid=(nj, t // tm),
        in_specs=[
            pl.BlockSpec((tm, d), lambda j, i: (i, 0)),
            pl.BlockSpec((tm, LANES), lambda j, i: (i, 0)),
            pl.BlockSpec((d, 1), lambda j, i: (0, 0)),
            pl.BlockSpec((None, d, tn), lambda j, i: (layer, 0, j)),
            pl.BlockSpec((None, d, tn), lambda j, i: (layer, 0, j + nj)),
        ],
        out_specs=pl.BlockSpec((tm, tn), lambda j, i: (i, j)),
        out_shape=jax.ShapeDtypeStruct((t, d_ff), BF16),
        scratch_shapes=[pltpu.VMEM((d, tn), BF16), pltpu.VMEM((d, tn), BF16)],
        compiler_params=_params("arbitrary", "arbitrary"),
        name="ffn_up",
    )(hb, rinv, norm_w.reshape(d, 1), w_gate_up, w_gate_up)


def _ffn(h, hb, rinv, norm_w, w_gate_up, w_down_b, layer, emit_norm):
    act = _ffn_up(hb, rinv, norm_w, w_gate_up, layer)
    return _mm_res(act, w_down_b, layer, h, emit_norm, name="ffn_down")


def _head_norm(acc, hw, scale):
    parts = []
    for c in range(acc.shape[1] // ATT_HEAD_DIM):
        part = acc[:, c * ATT_HEAD_DIM:(c + 1) * ATT_HEAD_DIM]
        parts.append(_rms_rows(part, hw) * scale)
    return jnp.concatenate(parts, axis=1)


def _qproj_kernel(hb_ref, rinv_ref, nw_ref, w_ref, hw_ref, o_ref, wb_ref, *, scale):
    @pl.when(pl.program_id(1) == 0)
    def _():
        _stage_weight(w_ref, nw_ref, wb_ref)

    for rows, acc in _projected_parts(hb_ref, rinv_ref, wb_ref):
        o_ref[:, rows] = _head_norm(acc, hw_ref[...], scale).T.astype(o_ref.dtype)


def _q_proj(hb, rinv, norm_w, w_layers, layer, head_w, scale, tm=1024, tn=512):
    t, d = hb.shape
    n = w_layers.shape[2]
    tm, tn = _tile(t, tm), _tile(n, tn)
    return pl.pallas_call(
        functools.partial(_qproj_kernel, scale=scale),
        grid=(n // tn, t // tm),
        in_specs=[
            pl.BlockSpec((tm, d), lambda j, i: (i, 0)),
            pl.BlockSpec((tm, LANES), lambda j, i: (i, 0)),
            pl.BlockSpec((d, 1), lambda j, i: (0, 0)),
            pl.BlockSpec((None, d, tn), lambda j, i: (layer, 0, j)),
            pl.BlockSpec((1, ATT_HEAD_DIM), lambda j, i: (0, 0)),
        ],
        out_specs=pl.BlockSpec((tn, tm), lambda j, i: (j, i)),
        out_shape=jax.ShapeDtypeStruct((n, t), BF16),
        scratch_shapes=[pltpu.VMEM((d, tn), BF16)],
        compiler_params=_params("arbitrary", "arbitrary"),
        name="q_proj",
    )(hb, rinv, norm_w.reshape(d, 1), w_layers, head_w.reshape(1, ATT_HEAD_DIM))


def _kvproj_kernel(hb_ref, rinv_ref, nw_ref, w_ref, wf_ref, bf_ref, hw_ref, tri_ref,
                   k_ref, vt_ref, cum_ref, wb_ref, wfb_ref, carry_ref, *, n_k_blocks, blocks_per_seq):
    j = pl.program_id(0)
    i = pl.program_id(1)
    tm = hb_ref.shape[0]

    @pl.when(i == 0)
    def _():
        _stage_weight(w_ref, nw_ref, wb_ref)

    @pl.when(j == 0)
    def _():
        @pl.when(i == 0)
        def _():
            _stage_weight(wf_ref, nw_ref, wfb_ref)

        logit = jnp.dot(hb_ref[...], wfb_ref[...], preferred_element_type=F32) * rinv_ref[...]
        log_f = -_softplus(-(logit + bf_ref[...]))

        @pl.when(i % blocks_per_seq == 0)
        def _():
            carry_ref[...] = jnp.zeros_like(carry_ref)

        rows_per_scan = tri_ref.shape[0]
        carry = carry_ref[0:1, :]
        for sb in range(tm // rows_per_scan):
            rows = slice(sb * rows_per_scan, (sb + 1) * rows_per_scan)
            cum = _cumsum_rows(log_f[rows, :], tri_ref) + carry
            cum_ref[rows, :] = jnp.concatenate(_bf16_pieces(-LOG2E * cum, BIAS_PIECES), axis=1)
            carry = cum[rows_per_scan - 1:rows_per_scan, :]
        carry_ref[...] = jnp.broadcast_to(carry, carry_ref.shape)

    @pl.when(j < n_k_blocks)
    def _():
        for rows, acc in _projected_parts(hb_ref, rinv_ref, wb_ref):
            k_ref[rows, :] = _head_norm(acc, hw_ref[...], 1.0).astype(k_ref.dtype)

    @pl.when(j >= n_k_blocks)
    def _():
        for rows, acc in _projected_parts(hb_ref, rinv_ref, wb_ref):
            vt_ref[:, rows] = acc.T.astype(vt_ref.dtype)


def _kv_proj(hb, rinv, norm_w, w_kvf, w_f, b_f, head_w, d_att, seq, tm=1024, tn=512, scan_rows=256):
    t, d = hb.shape
    n = 2 * d_att
    tm, tn = _tile(seq, tm), _tile(d_att, tn)
    ni = t // tm
    nk = d_att // tn
    tri = _scan_matrix(_tile(tm, scan_rows))
    kern = functools.partial(_kvproj_kernel, n_k_blocks=nk, blocks_per_seq=seq // tm)
    return pl.pallas_call(
        kern,
        grid=(n // tn, ni),
        in_specs=[
            pl.BlockSpec((tm, d), lambda j, i: (i, 0)),
            pl.BlockSpec((tm, LANES), lambda j, i: (i, 0)),
            pl.BlockSpec((d, 1), lambda j, i: (0, 0)),
            pl.BlockSpec((d, tn), lambda j, i: (0, j)),
            pl.BlockSpec((d, LANES), lambda j, i: (0, 0)),
            pl.BlockSpec((1, LANES), lambda j, i: (0, 0)),
            pl.BlockSpec((1, ATT_HEAD_DIM), lambda j, i: (0, 0)),
            pl.BlockSpec(tri.shape, lambda j, i: (0, 0)),
        ],
        out_specs=[
            pl.BlockSpec((tm, tn), lambda j, i: (jnp.where(j < nk, i, ni - 1), jnp.minimum(j, nk - 1))),
            pl.BlockSpec((tn, tm), lambda j, i: (jnp.maximum(j - nk, 0), jnp.where(j < nk, 0, i))),
            pl.BlockSpec((tm, BIAS_PIECES * LANES), lambda j, i: (jnp.where(j == 0, i, ni - 1), 0)),
        ],
        out_shape=[jax.ShapeDtypeStruct((t, d_att), BF16),
                   jax.ShapeDtypeStruct((d_att, t), BF16),
                   jax.ShapeDtypeStruct((t, BIAS_PIECES * LANES), BF16)],
        scratch_shapes=[pltpu.VMEM((d, tn), BF16), pltpu.VMEM((d, LANES), BF16),
                        pltpu.VMEM((SUBLANES, LANES), F32)],
        compiler_params=_params("arbitrary", "arbitrary"),
        name="kv_proj",
    )(hb, rinv, norm_w.reshape(d, 1), w_kvf, w_f, b_f, head_w.reshape(1, ATT_HEAD_DIM), tri)


BIAS_PIECES = SPLIT_PIECES


def _attn_kernel(qt_ref, k_ref, vt_ref, cum_ref, o_ref, kb_ref, *, tq):
    seq, hd = k_ref.shape
    nq = seq // tq
    head = pl.program_id(1)
    sub = lax.broadcasted_iota(jnp.int32, (hd, 1), 0)

    def ones_rows(width):
        return jnp.broadcast_to(jnp.where(sub < BIAS_PIECES, 1.0, 0.0), (hd, width)).astype(BF16)

    @pl.when(head == 0)
    def _():
        n_heads = kb_ref.shape[0]
        src = lax.broadcasted_iota(jnp.int32, (cum_ref.shape[1], n_heads * hd), 0)
        dst = lax.broadcasted_iota(jnp.int32, (cum_ref.shape[1], n_heads * hd), 1)
        dst_head, dst_lane = dst // hd, dst % hd
        pick = jnp.where((src == dst_lane * LANES + dst_head) & (dst_lane < BIAS_PIECES),
                         1.0, 0.0).astype(BF16)

        def prepare_rows(r, carry):
            rows = pl.ds(pl.multiple_of(r * tq, tq), tq)
            placed = jnp.dot(cum_ref[rows, :], pick, preferred_element_type=F32).astype(BF16)
            for hh in range(n_heads):
                kb_ref[hh, rows, :] = placed[:, hh * hd:(hh + 1) * hd]
            return carry

        lax.fori_loop(0, nq, prepare_rows, 0)

    tile_row = lax.broadcasted_iota(jnp.int32, (BF16_SUBLANES, 1), 0)

    def denom_rows(width):
        return jnp.broadcast_to(jnp.where(tile_row == 0, 1.0, 0.0), (BF16_SUBLANES, width)).astype(BF16)

    half = tq // 2
    def visible(n_keys, first_query):
        key_id = lax.broadcasted_iota(jnp.int32, (n_keys, half), 0)
        qry_id = lax.broadcasted_iota(jnp.int32, (n_keys, half), 1)
        return key_id <= qry_id + first_query

    diagonal_parts = ((half, slice(0, half), visible(half, 0)),
                      (tq, slice(half, tq), visible(tq, half)))
    full_parts = ((tq, slice(0, tq), None),)

    def scores(q0, start, parts):
        out = []
        for n_keys, lanes, _ in parts:
            keys = pl.ds(start, n_keys)
            width = lanes.stop - lanes.start
            k_aug = jnp.concatenate([k_ref[keys, :], kb_ref[head, keys, :]], axis=1)
            qt = jnp.concatenate([qt_ref[:, q0 + lanes.start:q0 + lanes.stop], ones_rows(width)],
                                 axis=0)
            out.append(jnp.dot(k_aug, qt, preferred_element_type=F32))
        return out

    def update(s_parts, start, carry, parts):
        m, acc = carry if carry is not None else (None, None)
        m_out, acc_out = [], []
        for s, (n_keys, lanes, mask) in zip(s_parts, parts):
            if mask is not None:
                s = jnp.where(mask, s, -jnp.inf)
            width = lanes.stop - lanes.start
            if carry is None:
                m_old, acc_old = jnp.full((SUBLANES, width), -jnp.inf, F32), None
            else:
                m_old, acc_old = m[:, lanes], acc[:, lanes]
            m_new = jnp.maximum(m_old, jnp.max(s, axis=0, keepdims=True))
            alpha = jnp.exp2(m_old - m_new)[0:1, :]
            p = jnp.exp2(s - m_new[0:1, :]).astype(BF16)
            vt_aug = jnp.concatenate([vt_ref[:, pl.ds(start, n_keys)], denom_rows(n_keys)], axis=0)
            pv = jnp.dot(vt_aug, p, preferred_element_type=F32)
            acc_out.append(pv if acc_old is None else alpha * acc_old + pv)
            m_out.append(m_new)
        return jnp.concatenate(m_out, axis=1), jnp.concatenate(acc_out, axis=1)

    work = [(qi, kb) for qi in range(nq) for kb in range(qi + 1)]

    def issue(item):
        qi, kb = item
        parts = diagonal_parts if kb == qi else full_parts
        return scores(qi * tq, kb * tq, parts), parts

    pending = issue(work[0])
    carry = None
    for idx, (qi, kb) in enumerate(work):
        s_cur, parts = pending
        if idx + 1 < len(work):
            pending = issue(work[idx + 1])
        carry = update(s_cur, kb * tq, carry, parts)
        if kb == qi:
            _, acc = carry
            o_ref[qi * tq:(qi + 1) * tq, :] = (acc[0:hd, :] / acc[hd:hd + 1, :]).T.astype(o_ref.dtype)
            carry = None


def _attention(q_t, k, v_t, cum, bsz, seq, n_heads, tq=512):
    t = k.shape[0]
    tq = _tile(seq, tq)
    hd = ATT_HEAD_DIM
    return pl.pallas_call(
        functools.partial(_attn_kernel, tq=tq),
        grid=(bsz, n_heads),
        in_specs=[
            pl.BlockSpec((hd, seq), lambda b, h: (h, b)),
            pl.BlockSpec((seq, hd), lambda b, h: (b, h)),
            pl.BlockSpec((hd, seq), lambda b, h: (h, b)),
            pl.BlockSpec((seq, cum.shape[1]), lambda b, h: (b, 0)),
        ],
        out_specs=pl.BlockSpec((seq, hd), lambda b, h: (b, h)),
        out_shape=jax.ShapeDtypeStruct((t, n_heads * hd), BF16),
        scratch_shapes=[pltpu.VMEM((n_heads, seq, hd), BF16)],
        compiler_params=_params("arbitrary", "arbitrary"),
        name="fox_attention",
    )(q_t, k, v_t, cum)


def kernel(x, a_norm_w, a_in_proj, a_conv_w, a_conv_b, a_dt_bias, a_A_log, a_D, a_gnorm_w, a_out_proj, kv_norm_w, w_kvf, b_f, k_norm_w, b_norm_w, w_q, q_norm_w, w_o, ffn_norm_w, w_gate_up, w_down):
    bsz, seq, d_model = x.shape
    t = bsz * seq
    n_a = a_norm_w.shape[0]
    n_b = b_norm_w.shape[0]
    depth = n_a + n_b
    assert n_a >= 1, "the attention layers take the bf16 stream a Mamba-2 layer emits"
    h = x.reshape(t, d_model).astype(F32)
    hb = rinv = None
    w_out_b = a_out_proj.astype(BF16)
    w_down_b = w_down.astype(BF16)
    w_o_b = w_o.astype(BF16)

    for i in range(n_a):
        n_ssm_heads = a_dt_bias.shape[1]
        n_main = a_in_proj.shape[2] - n_ssm_heads
        w_main = a_in_proj[i, :, :n_main].astype(BF16)
        w_dt = jnp.pad(a_in_proj[i, :, n_main:], ((0, 0), (0, LANES - n_ssm_heads)))
        zx, dt_raw = _in_proj(h, a_norm_w[i], w_main, w_dt)
        y = _ssd(zx, dt_raw, a_conv_w[i], a_conv_b[i], a_dt_bias[i], a_A_log[i], a_D[i],
                 a_gnorm_w[i], bsz, seq)
        h, hb, rinv = _mm_res(y, w_out_b, i, h, True, name="out_proj")
        more = i + 1 < depth
        h, *norm = _ffn(h, hb, rinv, ffn_norm_w[i], w_gate_up, w_down_b, i, more)
        hb, rinv = norm if more else (None, None)

    n_att_heads = b_f.shape[0]
    d_att = n_att_heads * ATT_HEAD_DIM
    scale = ATT_HEAD_DIM ** -0.5 * LOG2E
    k = v_t = cum = None
    for j in range(n_b):
        if j == 0:
            w_f = jnp.pad(w_kvf[:, 2 * d_att:], ((0, 0), (0, LANES - n_att_heads)))
            b_pad = jnp.pad(b_f.astype(F32), (0, LANES - n_att_heads)).reshape(1, LANES)
            k, v_t, cum = _kv_proj(hb, rinv, kv_norm_w, w_kvf, w_f, b_pad, k_norm_w, d_att, seq)
        q_t = _q_proj(hb, rinv, b_norm_w[j], w_q, j, q_norm_w[j], scale)
        o = _attention(q_t, k, v_t, cum, bsz, seq, n_att_heads)
        h, hb, rinv = _mm_res(o, w_o_b, j, h, True, name="attn_out")
        more = n_a + j + 1 < depth
        h, *norm = _ffn(h, hb, rinv, ffn_norm_w[n_a + j], w_gate_up, w_down_b, n_a + j, more)
        hb, rinv = norm if more else (None, None)

    return h.reshape(bsz, seq, d_model).astype(x.dtype)
```

```python
import functools

import jax
import jax.numpy as jnp
from jax import lax
from jax.experimental import pallas as pl
from jax.experimental.pallas import tpu as pltpu

F32 = jnp.float32
BF16 = jnp.bfloat16
EPS = 1e-6

SSM_HEAD_DIM = 64
SSM_GROUPS = 8
D_STATE = 128
CONV_WIDTH = 4
SSD_CHUNK = 128
ATT_HEAD_DIM = 128

LANES = 128
SUBLANES = 8
BF16_SUBLANES = 16
VMEM_LIMIT_BYTES = 56 * 1024 * 1024

CONV_HALO = BF16_SUBLANES
ROW_PARTS = 2
LOG2E = 1.4426950408889634

NT_DIMS = (((1,), (1,)), ((), ()))
TN_DIMS = (((0,), (0,)), ((), ()))


def _params(*semantics):
    return pltpu.CompilerParams(dimension_semantics=semantics,
                                vmem_limit_bytes=VMEM_LIMIT_BYTES)


def _silu(v):
    half = 0.5 * v
    return half + half * jnp.tanh(half)


def _softplus(v):
    return jnp.maximum(v, 0.0) + jnp.log1p(jnp.exp(-jnp.abs(v)))


def _rms_rows(v, w):
    ms = jnp.mean(v * v, axis=-1, keepdims=True)
    return v * lax.rsqrt(ms + EPS) * w


SPLIT_PIECES = 3


def _bf16_pieces(v, n):
    parts, rest = [], v
    for _ in range(n):
        part = rest.astype(BF16)
        parts.append(part)
        rest = rest - part.astype(F32)
    return parts


def _scan_matrix(rows):
    tri = jnp.arange(rows)[None, :] <= jnp.arange(rows)[:, None]
    return jnp.concatenate([tri] * SPLIT_PIECES, axis=1).astype(BF16)


def _cumsum_rows(v, scan_ref):
    stacked = jnp.concatenate(_bf16_pieces(v, SPLIT_PIECES), axis=0)
    return jnp.dot(scan_ref[...], stacked, preferred_element_type=F32)


def _tile(n, pref):
    t = min(n, pref)
    assert n % t == 0, (n, pref)
    return t


def _inproj_kernel(x_ref, nw_ref, w_ref, wdt_ref, o_ref, dt_ref, xn_ref):
    j = pl.program_id(1)

    @pl.when(j == 0)
    def _():
        w_dt = wdt_ref[...].astype(BF16)
        rs = x_ref.shape[0] // ROW_PARTS
        for part in range(ROW_PARTS):
            rows = slice(part * rs, (part + 1) * rs)
            xn = _rms_rows(x_ref[rows, :], nw_ref[...]).astype(BF16)
            xn_ref[rows, :] = xn
            dt_ref[rows, :] = jnp.dot(xn, w_dt, preferred_element_type=F32)
            o_ref[rows, :] = jnp.dot(xn, w_ref[...], preferred_element_type=F32).astype(o_ref.dtype)

    @pl.when(j > 0)
    def _():
        o_ref[...] = jnp.dot(xn_ref[...], w_ref[...],
                             preferred_element_type=F32).astype(o_ref.dtype)


def _in_proj(h, norm_w, w_main, w_dt, tm=1024, tn=1024):
    t, d = h.shape
    n = w_main.shape[1]
    tm, tn = _tile(t, tm), _tile(n, tn)
    return pl.pallas_call(
        _inproj_kernel,
        grid=(t // tm, n // tn),
        in_specs=[
            pl.BlockSpec((tm, d), lambda i, j: (i, 0)),
            pl.BlockSpec((1, d), lambda i, j: (0, 0)),
            pl.BlockSpec((d, tn), lambda i, j: (0, j)),
            pl.BlockSpec((d, LANES), lambda i, j: (0, 0)),
        ],
        out_specs=[
            pl.BlockSpec((tm, tn), lambda i, j: (i, j)),
            pl.BlockSpec((tm, LANES), lambda i, j: (i, 0)),
        ],
        out_shape=[jax.ShapeDtypeStruct((t, n), BF16),
                   jax.ShapeDtypeStruct((t, LANES), F32)],
        scratch_shapes=[pltpu.VMEM((tm, d), BF16)],
        compiler_params=_params("parallel", "arbitrary"),
        name="in_proj",
    )(h, norm_w.reshape(1, d), w_main, w_dt)


def _ssd_kernel(z_ref, x_ref, bc_ref, dt_ref, cwx_ref, cwbc_ref, cbx_ref, cbbc_ref,
                dtb_ref, alog_ref, dskip_ref, gnw_ref, expand_ref, tri_ref, shift_ref,
                y_ref,
                state_ref, xext_ref, bcext_ref, bcs_ref):
    L = SSD_CHUNK
    P = SSM_HEAD_DIM
    N = D_STATE
    G = SSM_GROUPS
    d_inner = x_ref.shape[1]
    d_bc = bc_ref.shape[1]
    heads_per_group = d_inner // (G * P)
    gw = heads_per_group * P
    halo = CONV_HALO

    @pl.when(pl.program_id(1) == 0)
    def _():
        state_ref[...] = jnp.zeros_like(state_ref)
        xext_ref[0:halo, :] = jnp.zeros((halo, d_inner), BF16)
        bcext_ref[0:halo, :] = jnp.zeros((halo, d_bc), BF16)

    xext_ref[halo:halo + L, :] = x_ref[...]
    bcext_ref[halo:halo + L, :] = bc_ref[...]

    def conv_silu(ext_ref, cur_ref, w_ref, b_ref, cols):
        shifted = jnp.dot(shift_ref[...], ext_ref[:, cols], preferred_element_type=F32)
        last = CONV_WIDTH - 1
        acc = 0.5 * b_ref[:, cols] + (0.5 * w_ref[last:last + 1, cols]) * cur_ref[:, cols].astype(F32)
        for k in range(last):
            acc = acc + (0.5 * w_ref[k:k + 1, cols]) * shifted[k * L:(k + 1) * L, :]
        return acc + acc * jnp.tanh(acc)

    for j in range(d_bc // gw):
        cols = slice(j * gw, (j + 1) * gw)
        bcs_ref[:, cols] = conv_silu(bcext_ref, bc_ref, cwbc_ref, cbbc_ref, cols).astype(BF16)
    bcext_ref[0:halo, :] = bcext_ref[L:L + halo, :]

    dtv = _softplus(dt_ref[...] + dtb_ref[...])
    a = dtv * (-LOG2E * jnp.exp(alog_ref[...]))
    acum = _cumsum_rows(a, tri_ref)
    a_last = acum[L - 1:L, :]
    acum_t = acum.T

    def split(v):
        return jnp.concatenate(_bf16_pieces(v, 2), axis=1)

    dt_hl = split(dtv)
    e_out_hl = split(jnp.exp2(acum))
    e_in_hl = split(dtv * jnp.exp2(a_last - acum))

    def expand(hl, cols):
        return jnp.dot(hl, expand_ref[:, cols], preferred_element_type=F32)

    row_id = lax.broadcasted_iota(jnp.int32, (L, L), 0)
    col_id = lax.broadcasted_iota(jnp.int32, (L, L), 1)
    causal = col_id <= row_id
    lane_id = lax.broadcasted_iota(jnp.int32, (L, 2 * P), 1)
    first_head = lane_id < P

    for g in range(G):
        gs = slice(g * gw, (g + 1) * gw)
        xs = conv_silu(xext_ref, x_ref, cwx_ref, cbx_ref, gs)
        xdt = (xs * expand(dt_hl, gs)).astype(BF16)
        xw = (xs * expand(e_in_hl, gs)).astype(BF16)
        e_out_x = expand(e_out_hl, gs)
        b_g = bcs_ref[:, g * N:(g + 1) * N]
        c_g = bcs_ref[:, G * N + g * N:G * N + (g + 1) * N]
        cb = lax.dot_general(c_g, b_g, NT_DIMS, preferred_element_type=F32)
        s_g = state_ref[:, gs]
        y_off = jnp.dot(c_g, s_g.astype(BF16), preferred_element_type=F32) * e_out_x
        upd = lax.dot_general(b_g, xw, TN_DIMS, preferred_element_type=F32)
        state_ref[:, gs] = s_g * e_out_x[L - 1:L, :] + upd
        y_pairs = []
        for pr in range(heads_per_group // 2):
            h0 = g * heads_per_group + 2 * pr
            ms = []
            for hh in (h0, h0 + 1):
                seg = acum[:, hh:hh + 1] - acum_t[hh:hh + 1, :]
                decay = jnp.exp2(jnp.where(causal, seg, -jnp.inf))
                ms.append((cb * decay).astype(BF16))
            m_cat = jnp.concatenate(ms, axis=1)
            xp = xdt[:, pr * 2 * P:(pr + 1) * 2 * P]
            zero = jnp.zeros_like(xp)
            rhs = jnp.concatenate([jnp.where(first_head, xp, zero),
                                   jnp.where(first_head, zero, xp)], axis=0)
            y_pairs.append(jnp.dot(m_cat, rhs, preferred_element_type=F32))
        y = jnp.concatenate(y_pairs, axis=1) + y_off + xs * dskip_ref[:, gs]
        y = y * _silu(z_ref[:, gs].astype(F32))
        y_ref[:, gs] = _rms_rows(y, gnw_ref[:, gs]).astype(y_ref.dtype)
    xext_ref[0:halo, :] = xext_ref[L:L + halo, :]


def _ssd(zx, dt_raw, conv_w, conv_b, dt_bias, a_log, d_skip, gnorm_w, bsz, seq):
    t = zx.shape[0]
    L, P, N, G = SSD_CHUNK, SSM_HEAD_DIM, D_STATE, SSM_GROUPS
    n_heads = dt_bias.shape[0]
    d_inner = n_heads * P
    d_bc = 2 * G * N
    assert zx.shape[1] == 2 * d_inner + d_bc and n_heads <= LANES
    assert d_inner % d_bc == 0 and seq % L == 0
    nc = seq // L
    bc_blk = 2 * d_inner // d_bc

    pad_h = LANES - n_heads
    dtb = jnp.pad(dt_bias.astype(F32), (0, pad_h)).reshape(1, LANES)
    alog = jnp.pad(a_log.astype(F32), (0, pad_h)).reshape(1, LANES)
    dskip_x = jnp.repeat(d_skip.astype(F32), P).reshape(1, d_inner)
    head_of_channel = jnp.arange(d_inner, dtype=jnp.int32) // P
    sel = (jnp.arange(LANES, dtype=jnp.int32)[:, None] == head_of_channel[None, :])
    expand_mat = jnp.concatenate([sel, sel], axis=0).astype(BF16)
    tri = _scan_matrix(L)
    n_shift = CONV_WIDTH - 1
    src_row = (CONV_HALO - n_shift + jnp.arange(n_shift)[:, None] + jnp.arange(L)[None, :]).reshape(-1)
    shift_mat = (src_row[:, None] == jnp.arange(CONV_HALO + L)[None, :]).astype(BF16)

    row = lambda b, c: b * nc + c
    const = lambda b, c: (0, 0)
    return pl.pallas_call(
        _ssd_kernel,
        grid=(bsz, nc),
        in_specs=[
            pl.BlockSpec((L, d_inner), lambda b, c: (row(b, c), 0)),
            pl.BlockSpec((L, d_inner), lambda b, c: (row(b, c), 1)),
            pl.BlockSpec((L, d_bc), lambda b, c: (row(b, c), bc_blk)),
            pl.BlockSpec((L, LANES), lambda b, c: (row(b, c), 0)),
            pl.BlockSpec((CONV_WIDTH, d_inner), const),
            pl.BlockSpec((CONV_WIDTH, d_bc), const),
            pl.BlockSpec((1, d_inner), const),
            pl.BlockSpec((1, d_bc), const),
            pl.BlockSpec((1, LANES), const),
            pl.BlockSpec((1, LANES), const),
            pl.BlockSpec((1, d_inner), const),
            pl.BlockSpec((1, d_inner), const),
            pl.BlockSpec((2 * LANES, d_inner), const),
            pl.BlockSpec(tri.shape, const),
            pl.BlockSpec((n_shift * L, CONV_HALO + L), const),
        ],
        out_specs=pl.BlockSpec((L, d_inner), lambda b, c: (row(b, c), 0)),
        out_shape=jax.ShapeDtypeStruct((t, d_inner), BF16),
        scratch_shapes=[
            pltpu.VMEM((N, d_inner), F32),
            pltpu.VMEM((CONV_HALO + L, d_inner), BF16),
            pltpu.VMEM((CONV_HALO + L, d_bc), BF16),
            pltpu.VMEM((L, d_bc), BF16),
        ],
        compiler_params=_params("parallel", "arbitrary"),
        name="ssd",
    )(zx, zx, zx, dt_raw,
      conv_w[:, :d_inner], conv_w[:, d_inner:],
      conv_b[:d_inner].reshape(1, d_inner), conv_b[d_inner:].reshape(1, d_bc),
      dtb, alog, dskip_x, gnorm_w.reshape(1, d_inner), expand_mat, tri, shift_mat)


def _mm_res_kernel(a_ref, w_ref, r_ref, h_ref, *norm_refs):
    h = r_ref[...] + jnp.dot(a_ref[...], w_ref[...], preferred_element_type=F32)
    h_ref[...] = h
    if norm_refs:
        hb_ref, rinv_ref = norm_refs
        hb_ref[...] = h.astype(BF16)
        ms = jnp.mean(h * h, axis=-1, keepdims=True)
        rinv_ref[...] = jnp.broadcast_to(lax.rsqrt(ms + EPS), rinv_ref.shape)


MM_RES_ROW_CHOICES = (512, 256)
MM_RES_VMEM_BUDGET = 50 * 1024 * 1024


def _mm_res_rows(t, k, n, emit_norm):
    for tm in MM_RES_ROW_CHOICES:
        tiles = tm * k * 2 + 2 * tm * n * 4 + (tm * n * 2 + tm * LANES * 4 if emit_norm else 0)
        if t % tm == 0 and k * n * 2 + 2 * tiles <= MM_RES_VMEM_BUDGET:
            return tm
    return _tile(t, MM_RES_ROW_CHOICES[-1])


def _mm_res(a, w_layers, layer, res, emit_norm, name="mm_res"):
    t, k = a.shape
    n = w_layers.shape[2]
    tm = _mm_res_rows(t, k, n, emit_norm)
    out_specs = [pl.BlockSpec((tm, n), lambda i: (i, 0))]
    out_shape = [jax.ShapeDtypeStruct((t, n), F32)]
    if emit_norm:
        out_specs += [pl.BlockSpec((tm, n), lambda i: (i, 0)),
                      pl.BlockSpec((tm, LANES), lambda i: (i, 0))]
        out_shape += [jax.ShapeDtypeStruct((t, n), BF16),
                      jax.ShapeDtypeStruct((t, LANES), F32)]
    return pl.pallas_call(
        _mm_res_kernel,
        grid=(t // tm,),
        in_specs=[
            pl.BlockSpec((tm, k), lambda i: (i, 0)),
            pl.BlockSpec((None, k, n), lambda i: (layer, 0, 0), pipeline_mode=pl.Buffered(1)),
            pl.BlockSpec((tm, n), lambda i: (i, 0)),
        ],
        out_specs=out_specs,
        out_shape=out_shape,
        compiler_params=_params("arbitrary"),
        name=name,
    )(a, w_layers, res)


def _stage_weight(w_ref, nw_ref, wb_ref):
    wb_ref[...] = (w_ref[...] * nw_ref[...]).astype(BF16)


def _scale_rows(acc, rinv):
    return jnp.concatenate([acc[:, c * LANES:(c + 1) * LANES] * rinv
                            for c in range(acc.shape[1] // LANES)], axis=1)


def _projected_parts(hb_ref, rinv_ref, wb_ref):
    rs = hb_ref.shape[0] // ROW_PARTS
    parts = [slice(p * rs, (p + 1) * rs) for p in range(ROW_PARTS)]
    accs = [jnp.dot(hb_ref[rows, :], wb_ref[...], preferred_element_type=F32) for rows in parts]
    return [(rows, _scale_rows(acc, rinv_ref[rows, :])) for rows, acc in zip(parts, accs)]


def _ffn_up_kernel(hb_ref, rinv_ref, nw_ref, wg_ref, wu_ref, o_ref, wgb_ref, wub_ref):
    @pl.when(pl.program_id(1) == 0)
    def _():
        _stage_weight(wg_ref, nw_ref, wgb_ref)
        _stage_weight(wu_ref, nw_ref, wub_ref)

    hb = hb_ref[...]
    rinv = rinv_ref[...]
    g = _scale_rows(jnp.dot(hb, wgb_ref[...], preferred_element_type=F32), rinv)
    u = _scale_rows(jnp.dot(hb, wub_ref[...], preferred_element_type=F32), rinv)
    o_ref[...] = (_silu(g) * u).astype(o_ref.dtype)


def _ffn_up(hb, rinv, norm_w, w_gate_up, layer, tm=1024, tn=512):
    t, d = hb.shape
    d_ff = w_gate_up.shape[2] // 2
    tm, tn = _tile(t, tm), _tile(d_ff, tn)
    nj = d_ff // tn
    return pl.pallas_call(
        _ffn_up_kernel,
        grid=(nj, t // tm),
        in_specs=[
            pl.BlockSpec((tm, d), lambda j, i: (i, 0)),
            pl.BlockSpec((tm, LANES), lambda j, i: (i, 0)),
            pl.BlockSpec((d, 1), lambda j, i: (0, 0)),
            pl.BlockSpec((None, d, tn), lambda j, i: (layer, 0, j)),
            pl.BlockSpec((None, d, tn), lambda j, i: (layer, 0, j + nj)),
        ],
        out_specs=pl.BlockSpec((tm, tn), lambda j, i: (i, j)),
        out_shape=jax.ShapeDtypeStruct((t, d_ff), BF16),
        scratch_shapes=[pltpu.VMEM((d, tn), BF16), pltpu.VMEM((d, tn), BF16)],
        compiler_params=_params("arbitrary", "arbitrary"),
        name="ffn_up",
    )(hb, rinv, norm_w.reshape(d, 1), w_gate_up, w_gate_up)


def _ffn(h, hb, rinv, norm_w, w_gate_up, w_down_b, layer, emit_norm):
    act = _ffn_up(hb, rinv, norm_w, w_gate_up, layer)
    return _mm_res(act, w_down_b, layer, h, emit_norm, name="ffn_down")


def _head_norm(acc, hw, scale):
    parts = []
    for c in range(acc.shape[1] // ATT_HEAD_DIM):
        part = acc[:, c * ATT_HEAD_DIM:(c + 1) * ATT_HEAD_DIM]
        parts.append(_rms_rows(part, hw) * scale)
    return jnp.concatenate(parts, axis=1)


def _qproj_kernel(hb_ref, rinv_ref, nw_ref, w_ref, hw_ref, o_ref, wb_ref, *, scale):
    @pl.when(pl.program_id(1) == 0)
    def _():
        _stage_weight(w_ref, nw_ref, wb_ref)

    for rows, acc in _projected_parts(hb_ref, rinv_ref, wb_ref):
        o_ref[:, rows] = _head_norm(acc, hw_ref[...], scale).T.astype(o_ref.dtype)


def _q_proj(hb, rinv, norm_w, w_layers, layer, head_w, scale, tm=1024, tn=512):
    t, d = hb.shape
    n = w_layers.shape[2]
    tm, tn = _tile(t, tm), _tile(n, tn)
    return pl.pallas_call(
        functools.partial(_qproj_kernel, scale=scale),
        grid=(n // tn, t // tm),
        in_specs=[
            pl.BlockSpec((tm, d), lambda j, i: (i, 0)),
            pl.BlockSpec((tm, LANES), lambda j, i: (i, 0)),
            pl.BlockSpec((d, 1), lambda j, i: (0, 0)),
            pl.BlockSpec((None, d, tn), lambda j, i: (layer, 0, j)),
            pl.BlockSpec((1, ATT_HEAD_DIM), lambda j, i: (0, 0)),
        ],
        out_specs=pl.BlockSpec((tn, tm), lambda j, i: (j, i)),
        out_shape=jax.ShapeDtypeStruct((n, t), BF16),
        scratch_shapes=[pltpu.VMEM((d, tn), BF16)],
        compiler_params=_params("arbitrary", "arbitrary"),
        name="q_proj",
    )(hb, rinv, norm_w.reshape(d, 1), w_layers, head_w.reshape(1, ATT_HEAD_DIM))


def _kvproj_kernel(hb_ref, rinv_ref, nw_ref, w_ref, wf_ref, bf_ref, hw_ref, tri_ref,
                   k_ref, vt_ref, cum_ref, wb_ref, wfb_ref, carry_ref, *, n_k_blocks, blocks_per_seq):
    j = pl.program_id(0)
    i = pl.program_id(1)
    tm = hb_ref.shape[0]

    @pl.when(i == 0)
    def _():
        _stage_weight(w_ref, nw_ref, wb_ref)
        _stage_weight(wf_ref, nw_ref, wfb_ref)
        carry_ref[...] = jnp.zeros_like(carry_ref)

    def write_keys(parts):
        for rows, acc in parts:
            k_ref[rows, :] = _head_norm(acc, hw_ref[...], 1.0).astype(k_ref.dtype)

    def gates():
        logit = jnp.dot(hb_ref[...], wfb_ref[...], preferred_element_type=F32) * rinv_ref[...]
        log_f = -_softplus(-(logit + bf_ref[...]))
        rows_per_scan = tri_ref.shape[0]
        carry = jnp.where(i % blocks_per_seq == 0, 0.0, carry_ref[0:1, :])
        for sb in range(tm // rows_per_scan):
            rows = slice(sb * rows_per_scan, (sb + 1) * rows_per_scan)
            cum = _cumsum_rows(log_f[rows, :], tri_ref) + carry
            cum_ref[rows, :] = jnp.concatenate(_bf16_pieces(-LOG2E * cum, BIAS_PIECES), axis=1)
            carry = cum[rows_per_scan - 1:rows_per_scan, :]
        carry_ref[...] = jnp.broadcast_to(carry, carry_ref.shape)

    @pl.when(j == 0)
    def _():
        parts = _projected_parts(hb_ref, rinv_ref, wb_ref)
        gates()
        write_keys(parts)

    @pl.when(jnp.logical_and(j > 0, j < n_k_blocks))
    def _():
        write_keys(_projected_parts(hb_ref, rinv_ref, wb_ref))

    @pl.when(j >= n_k_blocks)
    def _():
        for rows, acc in _projected_parts(hb_ref, rinv_ref, wb_ref):
            vt_ref[:, rows] = acc.T.astype(vt_ref.dtype)


def _kv_proj(hb, rinv, norm_w, w_kvf, w_f, b_f, head_w, d_att, seq, tm=1024, tn=512, scan_rows=256):
    t, d = hb.shape
    n = 2 * d_att
    tm, tn = _tile(seq, tm), _tile(d_att, tn)
    ni = t // tm
    nk = d_att // tn
    tri = _scan_matrix(_tile(tm, scan_rows))
    kern = functools.partial(_kvproj_kernel, n_k_blocks=nk, blocks_per_seq=seq // tm)
    return pl.pallas_call(
        kern,
        grid=(n // tn, ni),
        in_specs=[
            pl.BlockSpec((tm, d), lambda j, i: (i, 0)),
            pl.BlockSpec((tm, LANES), lambda j, i: (i, 0)),
            pl.BlockSpec((d, 1), lambda j, i: (0, 0)),
            pl.BlockSpec((d, tn), lambda j, i: (0, j)),
            pl.BlockSpec((d, LANES), lambda j, i: (0, 0)),
            pl.BlockSpec((1, LANES), lambda j, i: (0, 0)),
            pl.BlockSpec((1, ATT_HEAD_DIM), lambda j, i: (0, 0)),
            pl.BlockSpec(tri.shape, lambda j, i: (0, 0)),
        ],
        out_specs=[
            pl.BlockSpec((tm, tn), lambda j, i: (jnp.where(j < nk, i, ni - 1), jnp.minimum(j, nk - 1))),
            pl.BlockSpec((tn, tm), lambda j, i: (jnp.maximum(j - nk, 0), jnp.where(j < nk, 0, i))),
            pl.BlockSpec((tm, BIAS_PIECES * LANES), lambda j, i: (jnp.where(j == 0, i, ni - 1), 0)),
        ],
        out_shape=[jax.ShapeDtypeStruct((t, d_att), BF16),
                   jax.ShapeDtypeStruct((d_att, t), BF16),
                   jax.ShapeDtypeStruct((t, BIAS_PIECES * LANES), BF16)],
        scratch_shapes=[pltpu.VMEM((d, tn), BF16), pltpu.VMEM((d, LANES), BF16),
                        pltpu.VMEM((SUBLANES, LANES), F32)],
        compiler_params=_params("arbitrary", "arbitrary"),
        name="kv_proj",
    )(hb, rinv, norm_w.reshape(d, 1), w_kvf, w_f, b_f, head_w.reshape(1, ATT_HEAD_DIM), tri)


BIAS_PIECES = SPLIT_PIECES
SCORE_LOOKAHEAD = 1


def _attn_kernel(qt_ref, k_ref, vt_ref, cum_ref, o_ref, kb_ref, *, tq):
    seq, hd = k_ref.shape
    nq = seq // tq
    head = pl.program_id(1)
    sub = lax.broadcasted_iota(jnp.int32, (hd, 1), 0)

    def ones_rows(width):
        return jnp.broadcast_to(jnp.where(sub < BIAS_PIECES, 1.0, 0.0), (hd, width)).astype(BF16)

    @pl.when(head == 0)
    def _():
        n_heads = kb_ref.shape[0]
        src = lax.broadcasted_iota(jnp.int32, (cum_ref.shape[1], n_heads * hd), 0)
        dst = lax.broadcasted_iota(jnp.int32, (cum_ref.shape[1], n_heads * hd), 1)
        dst_head, dst_lane = dst // hd, dst % hd
        pick = jnp.where((src == dst_lane * LANES + dst_head) & (dst_lane < BIAS_PIECES),
                         1.0, 0.0).astype(BF16)

        def prepare_rows(r, carry):
            rows = pl.ds(pl.multiple_of(r * tq, tq), tq)
            placed = jnp.dot(cum_ref[rows, :], pick, preferred_element_type=F32).astype(BF16)
            for hh in range(n_heads):
                kb_ref[hh, rows, :] = placed[:, hh * hd:(hh + 1) * hd]
            return carry

        lax.fori_loop(0, nq, prepare_rows, 0)

    tile_row = lax.broadcasted_iota(jnp.int32, (BF16_SUBLANES, 1), 0)

    def denom_rows(width):
        return jnp.broadcast_to(jnp.where(tile_row == 0, 1.0, 0.0), (BF16_SUBLANES, width)).astype(BF16)

    half = tq // 2
    def visible(n_keys, first_query):
        key_id = lax.broadcasted_iota(jnp.int32, (n_keys, half), 0)
        qry_id = lax.broadcasted_iota(jnp.int32, (n_keys, half), 1)
        return key_id <= qry_id + first_query

    diagonal_parts = ((half, slice(0, half), visible(half, 0)),
                      (tq, slice(half, tq), visible(tq, half)))
    full_parts = ((tq, slice(0, tq), None),)

    def scores(q0, start, parts):
        out = []
        for n_keys, lanes, _ in parts:
            keys = pl.ds(start, n_keys)
            width = lanes.stop - lanes.start
            k_aug = jnp.concatenate([k_ref[keys, :], kb_ref[head, keys, :]], axis=1)
            qt = jnp.concatenate([qt_ref[:, q0 + lanes.start:q0 + lanes.stop], ones_rows(width)],
                                 axis=0)
            out.append(jnp.dot(k_aug, qt, preferred_element_type=F32))
        return out

    def update(s_parts, start, carry, parts):
        m, acc = carry if carry is not None else (None, None)
        m_out, acc_out = [], []
        for s, (n_keys, lanes, mask) in zip(s_parts, parts):
            if mask is not None:
                s = jnp.where(mask, s, -jnp.inf)
            width = lanes.stop - lanes.start
            if carry is None:
                m_old, acc_old = jnp.full((SUBLANES, width), -jnp.inf, F32), None
            else:
                m_old, acc_old = m[:, lanes], acc[:, lanes]
            m_new = jnp.maximum(m_old, jnp.max(s, axis=0, keepdims=True))
            alpha = jnp.exp2(m_old - m_new)[0:1, :]
            p = jnp.exp2(s - m_new[0:1, :]).astype(BF16)
            vt_aug = jnp.concatenate([vt_ref[:, pl.ds(start, n_keys)], denom_rows(n_keys)], axis=0)
            pv = jnp.dot(vt_aug, p, preferred_element_type=F32)
            acc_out.append(pv if acc_old is None else alpha * acc_old + pv)
            m_out.append(m_new)
        return jnp.concatenate(m_out, axis=1), jnp.concatenate(acc_out, axis=1)

    work = []
    lo, hi = 0, nq - 1
    while lo <= hi:
        chains = [[(q, kb) for kb in range(q + 1)] for q in sorted({lo, hi})]
        while any(chains):
            for chain in chains:
                if chain:
                    work.append(chain.pop(0))
        lo, hi = lo + 1, hi - 1

    def issue(item):
        qi, kb = item
        parts = diagonal_parts if kb == qi else full_parts
        return scores(qi * tq, kb * tq, parts), parts

    pending = [issue(item) for item in work[:SCORE_LOOKAHEAD]]
    carries = {}
    for idx, (qi, kb) in enumerate(work):
        s_cur, parts = pending.pop(0)
        if idx + SCORE_LOOKAHEAD < len(work):
            pending.append(issue(work[idx + SCORE_LOOKAHEAD]))
        carries[qi] = update(s_cur, kb * tq, carries.get(qi), parts)
        if kb == qi:
            _, acc = carries.pop(qi)
            o_ref[qi * tq:(qi + 1) * tq, :] = (acc[0:hd, :] / acc[hd:hd + 1, :]).T.astype(o_ref.dtype)


def _attention(q_t, k, v_t, cum, bsz, seq, n_heads, tq=512):
    t = k.shape[0]
    tq = _tile(seq, tq)
    hd = ATT_HEAD_DIM
    return pl.pallas_call(
        functools.partial(_attn_kernel, tq=tq),
        grid=(bsz, n_heads),
        in_specs=[
            pl.BlockSpec((hd, seq), lambda b, h: (h, b)),
            pl.BlockSpec((seq, hd), lambda b, h: (b, h)),
            pl.BlockSpec((hd, seq), lambda b, h: (h, b)),
            pl.BlockSpec((seq, cum.shape[1]), lambda b, h: (b, 0)),
        ],
        out_specs=pl.BlockSpec((seq, hd), lambda b, h: (b, h)),
        out_shape=jax.ShapeDtypeStruct((t, n_heads * hd), BF16),
        scratch_shapes=[pltpu.VMEM((n_heads, seq, hd), BF16)],
        compiler_params=_params("arbitrary", "arbitrary"),
        name="fox_attention",
    )(q_t, k, v_t, cum)


def kernel(x, a_norm_w, a_in_proj, a_conv_w, a_conv_b, a_dt_bias, a_A_log, a_D, a_gnorm_w, a_out_proj, kv_norm_w, w_kvf, b_f, k_norm_w, b_norm_w, w_q, q_norm_w, w_o, ffn_norm_w, w_gate_up, w_down):
    bsz, seq, d_model = x.shape
    t = bsz * seq
    n_a = a_norm_w.shape[0]
    n_b = b_norm_w.shape[0]
    depth = n_a + n_b
    assert n_a >= 1, "the attention layers take the bf16 stream a Mamba-2 layer emits"
    h = x.reshape(t, d_model).astype(F32)
    hb = rinv = None
    w_out_b = a_out_proj.astype(BF16)
    w_down_b = w_down.astype(BF16)
    w_o_b = w_o.astype(BF16)

    for i in range(n_a):
        n_ssm_heads = a_dt_bias.shape[1]
        n_main = a_in_proj.shape[2] - n_ssm_heads
        w_main = a_in_proj[i, :, :n_main].astype(BF16)
        w_dt = jnp.pad(a_in_proj[i, :, n_main:], ((0, 0), (0, LANES - n_ssm_heads)))
        zx, dt_raw = _in_proj(h, a_norm_w[i], w_main, w_dt)
        y = _ssd(zx, dt_raw, a_conv_w[i], a_conv_b[i], a_dt_bias[i], a_A_log[i], a_D[i],
                 a_gnorm_w[i], bsz, seq)
        h, hb, rinv = _mm_res(y, w_out_b, i, h, True, name="out_proj")
        more = i + 1 < depth
        h, *norm = _ffn(h, hb, rinv, ffn_norm_w[i], w_gate_up, w_down_b, i, more)
        hb, rinv = norm if more else (None, None)

    n_att_heads = b_f.shape[0]
    d_att = n_att_heads * ATT_HEAD_DIM
    scale = ATT_HEAD_DIM ** -0.5 * LOG2E
    k = v_t = cum = None
    for j in range(n_b):
        if j == 0:
            w_f = jnp.pad(w_kvf[:, 2 * d_att:], ((0, 0), (0, LANES - n_att_heads)))
            b_pad = jnp.pad(b_f.astype(F32), (0, LANES - n_att_heads)).reshape(1, LANES)
            k, v_t, cum = _kv_proj(hb, rinv, kv_norm_w, w_kvf, w_f, b_pad, k_norm_w, d_att, seq)
        q_t = _q_proj(hb, rinv, b_norm_w[j], w_q, j, q_norm_w[j], scale)
        o = _attention(q_t, k, v_t, cum, bsz, seq, n_att_heads)
        h, hb, rinv = _mm_res(o, w_o_b, j, h, True, name="attn_out")
        more = n_a + j + 1 < depth
        h, *norm = _ffn(h, hb, rinv, ffn_norm_w[n_a + j], w_gate_up, w_down_b, n_a + j, more)
        hb, rinv = norm if more else (None, None)

    return h.reshape(bsz, seq, d_model).astype(x.dtype)
```

```python
import functools

import jax
import jax.numpy as jnp
from jax import lax
from jax.experimental import pallas as pl
from jax.experimental.pallas import tpu as pltpu

F32 = jnp.float32
BF16 = jnp.bfloat16
EPS = 1e-6

SSM_HEAD_DIM = 64
SSM_GROUPS = 8
D_STATE = 128
CONV_WIDTH = 4
SSD_CHUNK = 128
ATT_HEAD_DIM = 128

LANES = 128
SUBLANES = 8
BF16_SUBLANES = 16
VMEM_LIMIT_BYTES = 56 * 1024 * 1024

CONV_HALO = BF16_SUBLANES
ROW_PARTS = 4
LOG2E = 1.4426950408889634

NT_DIMS = (((1,), (1,)), ((), ()))
TN_DIMS = (((0,), (0,)), ((), ()))


def _params(*semantics):
    return pltpu.CompilerParams(dimension_semantics=semantics,
                                vmem_limit_bytes=VMEM_LIMIT_BYTES)


def _silu(v):
    half = 0.5 * v
    return half + half * jnp.tanh(half)


def _softplus(v):
    return jnp.maximum(v, 0.0) + jnp.log1p(jnp.exp(-jnp.abs(v)))


def _rms_rows(v, w):
    ms = jnp.mean(v * v, axis=-1, keepdims=True)
    return v * lax.rsqrt(ms + EPS) * w


SPLIT_PIECES = 3


def _bf16_pieces(v, n):
    parts, rest = [], v
    for _ in range(n):
        part = rest.astype(BF16)
        parts.append(part)
        rest = rest - part.astype(F32)
    return parts


def _scan_matrix(rows):
    tri = jnp.arange(rows)[None, :] <= jnp.arange(rows)[:, None]
    return jnp.concatenate([tri] * SPLIT_PIECES, axis=1).astype(BF16)


def _cumsum_rows(v, scan_ref):
    stacked = jnp.concatenate(_bf16_pieces(v, SPLIT_PIECES), axis=0)
    return jnp.dot(scan_ref[...], stacked, preferred_element_type=F32)


def _tile(n, pref):
    t = min(n, pref)
    assert n % t == 0, (n, pref)
    return t


def _inproj_kernel(x_ref, nw_ref, w_ref, wdt_ref, o_ref, dt_ref, xn_ref):
    j = pl.program_id(1)

    @pl.when(j == 0)
    def _():
        w_dt = wdt_ref[...].astype(BF16)
        rs = x_ref.shape[0] // ROW_PARTS
        for part in range(ROW_PARTS):
            rows = slice(part * rs, (part + 1) * rs)
            xn = _rms_rows(x_ref[rows, :], nw_ref[...]).astype(BF16)
            xn_ref[rows, :] = xn
            dt_ref[rows, :] = jnp.dot(xn, w_dt, preferred_element_type=F32)
            o_ref[rows, :] = jnp.dot(xn, w_ref[...], preferred_element_type=F32).astype(o_ref.dtype)

    @pl.when(j > 0)
    def _():
        o_ref[...] = jnp.dot(xn_ref[...], w_ref[...],
                             preferred_element_type=F32).astype(o_ref.dtype)


def _in_proj(h, norm_w, w_main, w_dt, tm=1024, tn=1024):
    t, d = h.shape
    n = w_main.shape[1]
    tm, tn = _tile(t, tm), _tile(n, tn)
    return pl.pallas_call(
        _inproj_kernel,
        grid=(t // tm, n // tn),
        in_specs=[
            pl.BlockSpec((tm, d), lambda i, j: (i, 0)),
            pl.BlockSpec((1, d), lambda i, j: (0, 0)),
            pl.BlockSpec((d, tn), lambda i, j: (0, j)),
            pl.BlockSpec((d, LANES), lambda i, j: (0, 0)),
        ],
        out_specs=[
            pl.BlockSpec((tm, tn), lambda i, j: (i, j)),
            pl.BlockSpec((tm, LANES), lambda i, j: (i, 0)),
        ],
        out_shape=[jax.ShapeDtypeStruct((t, n), BF16),
                   jax.ShapeDtypeStruct((t, LANES), F32)],
        scratch_shapes=[pltpu.VMEM((tm, d), BF16)],
        compiler_params=_params("parallel", "arbitrary"),
        name="in_proj",
    )(h, norm_w.reshape(1, d), w_main, w_dt)


def _ssd_kernel(z_ref, x_ref, bc_ref, dt_ref, cwx_ref, cwbc_ref, cbx_ref, cbbc_ref,
                dtb_ref, alog_ref, dskip_ref, gnw_ref, expand_ref, tri_ref, shift_ref,
                y_ref,
                state_ref, xext_ref, bcext_ref, bcs_ref):
    L = SSD_CHUNK
    P = SSM_HEAD_DIM
    N = D_STATE
    G = SSM_GROUPS
    d_inner = x_ref.shape[1]
    d_bc = bc_ref.shape[1]
    heads_per_group = d_inner // (G * P)
    gw = heads_per_group * P
    halo = CONV_HALO

    @pl.when(pl.program_id(1) == 0)
    def _():
        state_ref[...] = jnp.zeros_like(state_ref)
        xext_ref[0:halo, :] = jnp.zeros((halo, d_inner), BF16)
        bcext_ref[0:halo, :] = jnp.zeros((halo, d_bc), BF16)

    xext_ref[halo:halo + L, :] = x_ref[...]
    bcext_ref[halo:halo + L, :] = bc_ref[...]

    def conv_silu(ext_ref, cur_ref, w_ref, b_ref, cols):
        shifted = jnp.dot(shift_ref[...], ext_ref[:, cols], preferred_element_type=F32)
        last = CONV_WIDTH - 1
        acc = 0.5 * b_ref[:, cols] + (0.5 * w_ref[last:last + 1, cols]) * cur_ref[:, cols].astype(F32)
        for k in range(last):
            acc = acc + (0.5 * w_ref[k:k + 1, cols]) * shifted[k * L:(k + 1) * L, :]
        return acc + acc * jnp.tanh(acc)

    for j in range(d_bc // gw):
        cols = slice(j * gw, (j + 1) * gw)
        bcs_ref[:, cols] = conv_silu(bcext_ref, bc_ref, cwbc_ref, cbbc_ref, cols).astype(BF16)
    bcext_ref[0:halo, :] = bcext_ref[L:L + halo, :]

    dtv = _softplus(dt_ref[...] + dtb_ref[...])
    a = dtv * (-LOG2E * jnp.exp(alog_ref[...]))
    acum = _cumsum_rows(a, tri_ref)
    a_last = acum[L - 1:L, :]
    acum_t = acum.T

    def split(v):
        return jnp.concatenate(_bf16_pieces(v, 2), axis=1)

    dt_hl = split(dtv)
    e_out_hl = split(jnp.exp2(acum))
    e_in_hl = split(dtv * jnp.exp2(a_last - acum))

    def expand(hl, cols):
        return jnp.dot(hl, expand_ref[:, cols], preferred_element_type=F32)

    row_id = lax.broadcasted_iota(jnp.int32, (L, L), 0)
    col_id = lax.broadcasted_iota(jnp.int32, (L, L), 1)
    causal = col_id <= row_id
    lane_id = lax.broadcasted_iota(jnp.int32, (L, 2 * P), 1)
    first_head = lane_id < P

    for g in range(G):
        gs = slice(g * gw, (g + 1) * gw)
        xs = conv_silu(xext_ref, x_ref, cwx_ref, cbx_ref, gs)
        xdt = (xs * expand(dt_hl, gs)).astype(BF16)
        xw = (xs * expand(e_in_hl, gs)).astype(BF16)
        e_out_x = expand(e_out_hl, gs)
        b_g = bcs_ref[:, g * N:(g + 1) * N]
        c_g = bcs_ref[:, G * N + g * N:G * N + (g + 1) * N]
        cb = lax.dot_general(c_g, b_g, NT_DIMS, preferred_element_type=F32)
        s_g = state_ref[:, gs]
        y_off = jnp.dot(c_g, s_g.astype(BF16), preferred_element_type=F32) * e_out_x
        upd = lax.dot_general(b_g, xw, TN_DIMS, preferred_element_type=F32)
        state_ref[:, gs] = s_g * e_out_x[L - 1:L, :] + upd
        y_pairs = []
        for pr in range(heads_per_group // 2):
            h0 = g * heads_per_group + 2 * pr
            ms = []
            for hh in (h0, h0 + 1):
                seg = acum[:, hh:hh + 1] - acum_t[hh:hh + 1, :]
                decay = jnp.exp2(jnp.where(causal, seg, -jnp.inf))
                ms.append((cb * decay).astype(BF16))
            m_cat = jnp.concatenate(ms, axis=1)
            xp = xdt[:, pr * 2 * P:(pr + 1) * 2 * P]
            zero = jnp.zeros_like(xp)
            rhs = jnp.concatenate([jnp.where(first_head, xp, zero),
                                   jnp.where(first_head, zero, xp)], axis=0)
            y_pairs.append(jnp.dot(m_cat, rhs, preferred_element_type=F32))
        y = jnp.concatenate(y_pairs, axis=1) + y_off + xs * dskip_ref[:, gs]
        y = y * _silu(z_ref[:, gs].astype(F32))
        y_ref[:, gs] = _rms_rows(y, gnw_ref[:, gs]).astype(y_ref.dtype)
    xext_ref[0:halo, :] = xext_ref[L:L + halo, :]


def _ssd(zx, dt_raw, conv_w, conv_b, dt_bias, a_log, d_skip, gnorm_w, bsz, seq):
    t = zx.shape[0]
    L, P, N, G = SSD_CHUNK, SSM_HEAD_DIM, D_STATE, SSM_GROUPS
    n_heads = dt_bias.shape[0]
    d_inner = n_heads * P
    d_bc = 2 * G * N
    assert zx.shape[1] == 2 * d_inner + d_bc and n_heads <= LANES
    assert d_inner % d_bc == 0 and seq % L == 0
    nc = seq // L
    bc_blk = 2 * d_inner // d_bc

    pad_h = LANES - n_heads
    dtb = jnp.pad(dt_bias.astype(F32), (0, pad_h)).reshape(1, LANES)
    alog = jnp.pad(a_log.astype(F32), (0, pad_h)).reshape(1, LANES)
    dskip_x = jnp.repeat(d_skip.astype(F32), P).reshape(1, d_inner)
    head_of_channel = jnp.arange(d_inner, dtype=jnp.int32) // P
    sel = (jnp.arange(LANES, dtype=jnp.int32)[:, None] == head_of_channel[None, :])
    expand_mat = jnp.concatenate([sel, sel], axis=0).astype(BF16)
    tri = _scan_matrix(L)
    n_shift = CONV_WIDTH - 1
    src_row = (CONV_HALO - n_shift + jnp.arange(n_shift)[:, None] + jnp.arange(L)[None, :]).reshape(-1)
    shift_mat = (src_row[:, None] == jnp.arange(CONV_HALO + L)[None, :]).astype(BF16)

    row = lambda b, c: b * nc + c
    const = lambda b, c: (0, 0)
    return pl.pallas_call(
        _ssd_kernel,
        grid=(bsz, nc),
        in_specs=[
            pl.BlockSpec((L, d_inner), lambda b, c: (row(b, c), 0)),
            pl.BlockSpec((L, d_inner), lambda b, c: (row(b, c), 1)),
            pl.BlockSpec((L, d_bc), lambda b, c: (row(b, c), bc_blk)),
            pl.BlockSpec((L, LANES), lambda b, c: (row(b, c), 0)),
            pl.BlockSpec((CONV_WIDTH, d_inner), const),
            pl.BlockSpec((CONV_WIDTH, d_bc), const),
            pl.BlockSpec((1, d_inner), const),
            pl.BlockSpec((1, d_bc), const),
            pl.BlockSpec((1, LANES), const),
            pl.BlockSpec((1, LANES), const),
            pl.BlockSpec((1, d_inner), const),
            pl.BlockSpec((1, d_inner), const),
            pl.BlockSpec((2 * LANES, d_inner), const),
            pl.BlockSpec(tri.shape, const),
            pl.BlockSpec((n_shift * L, CONV_HALO + L), const),
        ],
        out_specs=pl.BlockSpec((L, d_inner), lambda b, c: (row(b, c), 0)),
        out_shape=jax.ShapeDtypeStruct((t, d_inner), BF16),
        scratch_shapes=[
            pltpu.VMEM((N, d_inner), F32),
            pltpu.VMEM((CONV_HALO + L, d_inner), BF16),
            pltpu.VMEM((CONV_HALO + L, d_bc), BF16),
            pltpu.VMEM((L, d_bc), BF16),
        ],
        compiler_params=_params("parallel", "arbitrary"),
        name="ssd",
    )(zx, zx, zx, dt_raw,
      conv_w[:, :d_inner], conv_w[:, d_inner:],
      conv_b[:d_inner].reshape(1, d_inner), conv_b[d_inner:].reshape(1, d_bc),
      dtb, alog, dskip_x, gnorm_w.reshape(1, d_inner), expand_mat, tri, shift_mat)


def _mm_res_kernel(a_ref, w_ref, r_ref, h_ref, *norm_refs):
    h = r_ref[...] + jnp.dot(a_ref[...], w_ref[...], preferred_element_type=F32)
    h_ref[...] = h
    if norm_refs:
        hb_ref, rinv_ref = norm_refs
        hb_ref[...] = h.astype(BF16)
        ms = jnp.mean(h * h, axis=-1, keepdims=True)
        rinv_ref[...] = jnp.broadcast_to(lax.rsqrt(ms + EPS), rinv_ref.shape)


MM_RES_ROW_CHOICES = (512, 256)
MM_RES_VMEM_BUDGET = 50 * 1024 * 1024


def _mm_res_rows(t, k, n, emit_norm):
    for tm in MM_RES_ROW_CHOICES:
        tiles = tm * k * 2 + 2 * tm * n * 4 + (tm * n * 2 + tm * LANES * 4 if emit_norm else 0)
        if t % tm == 0 and k * n * 2 + 2 * tiles <= MM_RES_VMEM_BUDGET:
            return tm
    return _tile(t, MM_RES_ROW_CHOICES[-1])


def _mm_res(a, w_layers, layer, res, emit_norm, name="mm_res"):
    t, k = a.shape
    n = w_layers.shape[2]
    tm = _mm_res_rows(t, k, n, emit_norm)
    out_specs = [pl.BlockSpec((tm, n), lambda i: (i, 0))]
    out_shape = [jax.ShapeDtypeStruct((t, n), F32)]
    if emit_norm:
        out_specs += [pl.BlockSpec((tm, n), lambda i: (i, 0)),
                      pl.BlockSpec((tm, LANES), lambda i: (i, 0))]
        out_shape += [jax.ShapeDtypeStruct((t, n), BF16),
                      jax.ShapeDtypeStruct((t, LANES), F32)]
    return pl.pallas_call(
        _mm_res_kernel,
        grid=(t // tm,),
        in_specs=[
            pl.BlockSpec((tm, k), lambda i: (i, 0)),
            pl.BlockSpec((None, k, n), lambda i: (layer, 0, 0), pipeline_mode=pl.Buffered(1)),
            pl.BlockSpec((tm, n), lambda i: (i, 0)),
        ],
        out_specs=out_specs,
        out_shape=out_shape,
        compiler_params=_params("arbitrary"),
        name=name,
    )(a, w_layers, res)


def _stage_weight(w_ref, nw_ref, wb_ref):
    wb_ref[...] = (w_ref[...] * nw_ref[...]).astype(BF16)


def _scale_rows(acc, rinv):
    return jnp.concatenate([acc[:, c * LANES:(c + 1) * LANES] * rinv
                            for c in range(acc.shape[1] // LANES)], axis=1)


def _projected_parts(hb_ref, rinv_ref, wb_ref):
    rs = hb_ref.shape[0] // ROW_PARTS
    parts = [slice(p * rs, (p + 1) * rs) for p in range(ROW_PARTS)]
    accs = [jnp.dot(hb_ref[rows, :], wb_ref[...], preferred_element_type=F32) for rows in parts]
    return [(rows, _scale_rows(acc, rinv_ref[rows, :])) for rows, acc in zip(parts, accs)]


def _ffn_up_kernel(hb_ref, rinv_ref, nw_ref, wg_ref, wu_ref, o_ref, wgb_ref, wub_ref):
    @pl.when(pl.program_id(1) == 0)
    def _():
        _stage_weight(wg_ref, nw_ref, wgb_ref)
        _stage_weight(wu_ref, nw_ref, wub_ref)

    hb = hb_ref[...]
    rinv = rinv_ref[...]
    g = _scale_rows(jnp.dot(hb, wgb_ref[...], preferred_element_type=F32), rinv)
    u = _scale_rows(jnp.dot(hb, wub_ref[...], preferred_element_type=F32), rinv)
    o_ref[...] = (_silu(g) * u).astype(o_ref.dtype)


def _ffn_up(hb, rinv, norm_w, w_gate_up, layer, tm=1024, tn=512):
    t, d = hb.shape
    d_ff = w_gate_up.shape[2] // 2
    tm, tn = _tile(t, tm), _tile(d_ff, tn)
    nj = d_ff // tn
    return pl.pallas_call(
        _ffn_up_kernel,
        grid=(nj, t // tm),
        in_specs=[
            pl.BlockSpec((tm, d), lambda j, i: (i, 0)),
            pl.BlockSpec((tm, LANES), lambda j, i: (i, 0)),
            pl.BlockSpec((d, 1), lambda j, i: (0, 0)),
            pl.BlockSpec((None, d, tn), lambda j, i: (layer, 0, j)),
            pl.BlockSpec((None, d, tn), lambda j, i: (layer, 0, j + nj)),
        ],
        out_specs=pl.BlockSpec((tm, tn), lambda j, i: (i, j)),
        out_shape=jax.ShapeDtypeStruct((t, d_ff), BF16),
        scratch_shapes=[pltpu.VMEM((d, tn), BF16), pltpu.VMEM((d, tn), BF16)],
        compiler_params=_params("arbitrary", "arbitrary"),
        name="ffn_up",
    )(hb, rinv, norm_w.reshape(d, 1), w_gate_up, w_gate_up)


def _ffn(h, hb, rinv, norm_w, w_gate_up, w_down_b, layer, emit_norm):
    act = _ffn_up(hb, rinv, norm_w, w_gate_up, layer)
    return _mm_res(act, w_down_b, layer, h, emit_norm, name="ffn_down")


def _head_norm(acc, hw, scale):
    parts = []
    for c in range(acc.shape[1] // ATT_HEAD_DIM):
        part = acc[:, c * ATT_HEAD_DIM:(c + 1) * ATT_HEAD_DIM]
        parts.append(_rms_rows(part, hw) * scale)
    return jnp.concatenate(parts, axis=1)


def _qproj_kernel(hb_ref, rinv_ref, nw_ref, w_ref, hw_ref, o_ref, wb_ref, *, scale):
    @pl.when(pl.program_id(1) == 0)
    def _():
        _stage_weight(w_ref, nw_ref, wb_ref)

    for rows, acc in _projected_parts(hb_ref, rinv_ref, wb_ref):
        o_ref[:, rows] = _head_norm(acc, hw_ref[...], scale).T.astype(o_ref.dtype)


def _q_proj(hb, rinv, norm_w, w_layers, layer, head_w, scale, tm=1024, tn=512):
    t, d = hb.shape
    n = w_layers.shape[2]
    tm, tn = _tile(t, tm), _tile(n, tn)
    return pl.pallas_call(
        functools.partial(_qproj_kernel, scale=scale),
        grid=(n // tn, t // tm),
        in_specs=[
            pl.BlockSpec((tm, d), lambda j, i: (i, 0)),
            pl.BlockSpec((tm, LANES), lambda j, i: (i, 0)),
            pl.BlockSpec((d, 1), lambda j, i: (0, 0)),
            pl.BlockSpec((None, d, tn), lambda j, i: (layer, 0, j)),
            pl.BlockSpec((1, ATT_HEAD_DIM), lambda j, i: (0, 0)),
        ],
        out_specs=pl.BlockSpec((tn, tm), lambda j, i: (j, i)),
        out_shape=jax.ShapeDtypeStruct((n, t), BF16),
        scratch_shapes=[pltpu.VMEM((d, tn), BF16)],
        compiler_params=_params("arbitrary", "arbitrary"),
        name="q_proj",
    )(hb, rinv, norm_w.reshape(d, 1), w_layers, head_w.reshape(1, ATT_HEAD_DIM))


def _kvproj_kernel(hb_ref, rinv_ref, nw_ref, w_ref, wf_ref, bf_ref, hw_ref, tri_ref,
                   k_ref, vt_ref, cum_ref, wb_ref, wfb_ref, carry_ref, *, n_k_blocks, blocks_per_seq):
    j = pl.program_id(0)
    i = pl.program_id(1)
    tm = hb_ref.shape[0]

    @pl.when(i == 0)
    def _():
        _stage_weight(w_ref, nw_ref, wb_ref)
        _stage_weight(wf_ref, nw_ref, wfb_ref)
        carry_ref[...] = jnp.zeros_like(carry_ref)

    def write_keys(parts):
        for rows, acc in parts:
            k_ref[rows, :] = _head_norm(acc, hw_ref[...], 1.0).astype(k_ref.dtype)

    def gates():
        logit = jnp.dot(hb_ref[...], wfb_ref[...], preferred_element_type=F32) * rinv_ref[...]
        log_f = -_softplus(-(logit + bf_ref[...]))
        rows_per_scan = tri_ref.shape[0]
        carry = jnp.where(i % blocks_per_seq == 0, 0.0, carry_ref[0:1, :])
        for sb in range(tm // rows_per_scan):
            rows = slice(sb * rows_per_scan, (sb + 1) * rows_per_scan)
            cum = _cumsum_rows(log_f[rows, :], tri_ref) + carry
            cum_ref[rows, :] = jnp.concatenate(_bf16_pieces(-LOG2E * cum, BIAS_PIECES), axis=1)
            carry = cum[rows_per_scan - 1:rows_per_scan, :]
        carry_ref[...] = jnp.broadcast_to(carry, carry_ref.shape)

    @pl.when(j == 0)
    def _():
        parts = _projected_parts(hb_ref, rinv_ref, wb_ref)
        gates()
        write_keys(parts)

    @pl.when(jnp.logical_and(j > 0, j < n_k_blocks))
    def _():
        write_keys(_projected_parts(hb_ref, rinv_ref, wb_ref))

    @pl.when(j >= n_k_blocks)
    def _():
        for rows, acc in _projected_parts(hb_ref, rinv_ref, wb_ref):
            vt_ref[:, rows] = acc.T.astype(vt_ref.dtype)


def _kv_proj(hb, rinv, norm_w, w_kvf, w_f, b_f, head_w, d_att, seq, tm=1024, tn=512, scan_rows=256):
    t, d = hb.shape
    n = 2 * d_att
    tm, tn = _tile(seq, tm), _tile(d_att, tn)
    ni = t // tm
    nk = d_att // tn
    tri = _scan_matrix(_tile(tm, scan_rows))
    kern = functools.partial(_kvproj_kernel, n_k_blocks=nk, blocks_per_seq=seq // tm)
    return pl.pallas_call(
        kern,
        grid=(n // tn, ni),
        in_specs=[
            pl.BlockSpec((tm, d), lambda j, i: (i, 0)),
            pl.BlockSpec((tm, LANES), lambda j, i: (i, 0)),
            pl.BlockSpec((d, 1), lambda j, i: (0, 0)),
            pl.BlockSpec((d, tn), lambda j, i: (0, j)),
            pl.BlockSpec((d, LANES), lambda j, i: (0, 0)),
            pl.BlockSpec((1, LANES), lambda j, i: (0, 0)),
            pl.BlockSpec((1, ATT_HEAD_DIM), lambda j, i: (0, 0)),
            pl.BlockSpec(tri.shape, lambda j, i: (0, 0)),
        ],
        out_specs=[
            pl.BlockSpec((tm, tn), lambda j, i: (jnp.where(j < nk, i, ni - 1), jnp.minimum(j, nk - 1))),
            pl.BlockSpec((tn, tm), lambda j, i: (jnp.maximum(j - nk, 0), jnp.where(j < nk, 0, i))),
            pl.BlockSpec((tm, BIAS_PIECES * LANES), lambda j, i: (jnp.where(j == 0, i, ni - 1), 0)),
        ],
        out_shape=[jax.ShapeDtypeStruct((t, d_att), BF16),
                   jax.ShapeDtypeStruct((d_att, t), BF16),
                   jax.ShapeDtypeStruct((t, BIAS_PIECES * LANES), BF16)],
        scratch_shapes=[pltpu.VMEM((d, tn), BF16), pltpu.VMEM((d, LANES), BF16),
                        pltpu.VMEM((SUBLANES, LANES), F32)],
        compiler_params=_params("arbitrary", "arbitrary"),
        name="kv_proj",
    )(hb, rinv, norm_w.reshape(d, 1), w_kvf, w_f, b_f, head_w.reshape(1, ATT_HEAD_DIM), tri)


BIAS_PIECES = SPLIT_PIECES
SCORE_LOOKAHEAD = 1


def _attn_kernel(qt_ref, k_ref, vt_ref, cum_ref, o_ref, kb_ref, *, tq):
    seq, hd = k_ref.shape
    nq = seq // tq
    head = pl.program_id(1)
    sub = lax.broadcasted_iota(jnp.int32, (hd, 1), 0)

    def ones_rows(width):
        return jnp.broadcast_to(jnp.where(sub < BIAS_PIECES, 1.0, 0.0), (hd, width)).astype(BF16)

    @pl.when(head == 0)
    def _():
        n_heads = kb_ref.shape[0]
        src = lax.broadcasted_iota(jnp.int32, (cum_ref.shape[1], n_heads * hd), 0)
        dst = lax.broadcasted_iota(jnp.int32, (cum_ref.shape[1], n_heads * hd), 1)
        dst_head, dst_lane = dst // hd, dst % hd
        pick = jnp.where((src == dst_lane * LANES + dst_head) & (dst_lane < BIAS_PIECES),
                         1.0, 0.0).astype(BF16)

        def prepare_rows(r, carry):
            rows = pl.ds(pl.multiple_of(r * tq, tq), tq)
            placed = jnp.dot(cum_ref[rows, :], pick, preferred_element_type=F32).astype(BF16)
            for hh in range(n_heads):
                kb_ref[hh, rows, :] = placed[:, hh * hd:(hh + 1) * hd]
            return carry

        lax.fori_loop(0, nq, prepare_rows, 0)

    tile_row = lax.broadcasted_iota(jnp.int32, (BF16_SUBLANES, 1), 0)

    def denom_rows(width):
        return jnp.broadcast_to(jnp.where(tile_row == 0, 1.0, 0.0), (BF16_SUBLANES, width)).astype(BF16)

    half = tq // 2
    def visible(n_keys, first_query):
        key_id = lax.broadcasted_iota(jnp.int32, (n_keys, half), 0)
        qry_id = lax.broadcasted_iota(jnp.int32, (n_keys, half), 1)
        return key_id <= qry_id + first_query

    diagonal_parts = ((half, slice(0, half), visible(half, 0)),
                      (tq, slice(half, tq), visible(tq, half)))
    full_parts = ((tq, slice(0, tq), None),)

    def scores(q0, start, parts):
        out = []
        for n_keys, lanes, _ in parts:
            keys = pl.ds(start, n_keys)
            width = lanes.stop - lanes.start
            k_aug = jnp.concatenate([k_ref[keys, :], kb_ref[head, keys, :]], axis=1)
            qt = jnp.concatenate([qt_ref[:, q0 + lanes.start:q0 + lanes.stop], ones_rows(width)],
                                 axis=0)
            out.append(jnp.dot(k_aug, qt, preferred_element_type=F32))
        return out

    def update(s_parts, start, carry, parts):
        m, acc = carry if carry is not None else (None, None)
        m_out, acc_out = [], []
        for s, (n_keys, lanes, mask) in zip(s_parts, parts):
            if mask is not None:
                s = jnp.where(mask, s, -jnp.inf)
            width = lanes.stop - lanes.start
            if carry is None:
                m_old, acc_old = jnp.full((SUBLANES, width), -jnp.inf, F32), None
            else:
                m_old, acc_old = m[:, lanes], acc[:, lanes]
            m_new = jnp.maximum(m_old, jnp.max(s, axis=0, keepdims=True))
            alpha = jnp.exp2(m_old - m_new)[0:1, :]
            p = jnp.exp2(s - m_new[0:1, :]).astype(BF16)
            vt_aug = jnp.concatenate([vt_ref[:, pl.ds(start, n_keys)], denom_rows(n_keys)], axis=0)
            pv = jnp.dot(vt_aug, p, preferred_element_type=F32)
            acc_out.append(pv if acc_old is None else alpha * acc_old + pv)
            m_out.append(m_new)
        return jnp.concatenate(m_out, axis=1), jnp.concatenate(acc_out, axis=1)

    work = []
    lo, hi = 0, nq - 1
    while lo <= hi:
        chains = [[(q, kb) for kb in range(q + 1)] for q in sorted({lo, hi})]
        while any(chains):
            for chain in chains:
                if chain:
                    work.append(chain.pop(0))
        lo, hi = lo + 1, hi - 1

    def issue(item):
        qi, kb = item
        parts = diagonal_parts if kb == qi else full_parts
        return scores(qi * tq, kb * tq, parts), parts

    pending = [issue(item) for item in work[:SCORE_LOOKAHEAD]]
    carries = {}
    for idx, (qi, kb) in enumerate(work):
        s_cur, parts = pending.pop(0)
        if idx + SCORE_LOOKAHEAD < len(work):
            pending.append(issue(work[idx + SCORE_LOOKAHEAD]))
        carries[qi] = update(s_cur, kb * tq, carries.get(qi), parts)
        if kb == qi:
            _, acc = carries.pop(qi)
            o_ref[qi * tq:(qi + 1) * tq, :] = (acc[0:hd, :] / acc[hd:hd + 1, :]).T.astype(o_ref.dtype)


def _attention(q_t, k, v_t, cum, bsz, seq, n_heads, tq=512):
    t = k.shape[0]
    tq = _tile(seq, tq)
    hd = ATT_HEAD_DIM
    return pl.pallas_call(
        functools.partial(_attn_kernel, tq=tq),
        grid=(bsz, n_heads),
        in_specs=[
            pl.BlockSpec((hd, seq), lambda b, h: (h, b)),
            pl.BlockSpec((seq, hd), lambda b, h: (b, h)),
            pl.BlockSpec((hd, seq), lambda b, h: (h, b)),
            pl.BlockSpec((seq, cum.shape[1]), lambda b, h: (b, 0)),
        ],
        out_specs=pl.BlockSpec((seq, hd), lambda b, h: (b, h)),
        out_shape=jax.ShapeDtypeStruct((t, n_heads * hd), BF16),
        scratch_shapes=[pltpu.VMEM((n_heads, seq, hd), BF16)],
        compiler_params=_params("arbitrary", "arbitrary"),
        name="fox_attention",
    )(q_t, k, v_t, cum)


def kernel(x, a_norm_w, a_in_proj, a_conv_w, a_conv_b, a_dt_bias, a_A_log, a_D, a_gnorm_w, a_out_proj, kv_norm_w, w_kvf, b_f, k_norm_w, b_norm_w, w_q, q_norm_w, w_o, ffn_norm_w, w_gate_up, w_down):
    bsz, seq, d_model = x.shape
    t = bsz * seq
    n_a = a_norm_w.shape[0]
    n_b = b_norm_w.shape[0]
    depth = n_a + n_b
    assert n_a >= 1, "the attention layers take the bf16 stream a Mamba-2 layer emits"
    h = x.reshape(t, d_model).astype(F32)
    hb = rinv = None
    w_out_b = a_out_proj.astype(BF16)
    w_down_b = w_down.astype(BF16)
    w_o_b = w_o.astype(BF16)

    for i in range(n_a):
        n_ssm_heads = a_dt_bias.shape[1]
        n_main = a_in_proj.shape[2] - n_ssm_heads
        w_main = a_in_proj[i, :, :n_main].astype(BF16)
        w_dt = jnp.pad(a_in_proj[i, :, n_main:], ((0, 0), (0, LANES - n_ssm_heads)))
        zx, dt_raw = _in_proj(h, a_norm_w[i], w_main, w_dt)
        y = _ssd(zx, dt_raw, a_conv_w[i], a_conv_b[i], a_dt_bias[i], a_A_log[i], a_D[i],
                 a_gnorm_w[i], bsz, seq)
        h, hb, rinv = _mm_res(y, w_out_b, i, h, True, name="out_proj")
        more = i + 1 < depth
        h, *norm = _ffn(h, hb, rinv, ffn_norm_w[i], w_gate_up, w_down_b, i, more)
        hb, rinv = norm if more else (None, None)

    n_att_heads = b_f.shape[0]
    d_att = n_att_heads * ATT_HEAD_DIM
    scale = ATT_HEAD_DIM ** -0.5 * LOG2E
    k = v_t = cum = None
    for j in range(n_b):
        if j == 0:
            w_f = jnp.pad(w_kvf[:, 2 * d_att:], ((0, 0), (0, LANES - n_att_heads)))
            b_pad = jnp.pad(b_f.astype(F32), (0, LANES - n_att_heads)).reshape(1, LANES)
            k, v_t, cum = _kv_proj(hb, rinv, kv_norm_w, w_kvf, w_f, b_pad, k_norm_w, d_att, seq)
        q_t = _q_proj(hb, rinv, b_norm_w[j], w_q, j, q_norm_w[j], scale)
        o = _attention(q_t, k, v_t, cum, bsz, seq, n_att_heads)
        h, hb, rinv = _mm_res(o, w_o_b, j, h, True, name="attn_out")
        more = n_a + j + 1 < depth
        h, *norm = _ffn(h, hb, rinv, ffn_norm_w[n_a + j], w_gate_up, w_down_b, n_a + j, more)
        hb, rinv = norm if more else (None, None)

    return h.reshape(bsz, seq, d_model).astype(x.dtype)
```

```python
import functools

import jax
import jax.numpy as jnp
from jax import lax
from jax.experimental import pallas as pl
from jax.experimental.pallas import tpu as pltpu

F32 = jnp.float32
BF16 = jnp.bfloat16
EPS = 1e-6

SSM_HEAD_DIM = 64
SSM_GROUPS = 8
D_STATE = 128
CONV_WIDTH = 4
SSD_CHUNK = 128
ATT_HEAD_DIM = 128

LANES = 128
SUBLANES = 8
BF16_SUBLANES = 16
VMEM_LIMIT_BYTES = 56 * 1024 * 1024

CONV_HALO = BF16_SUBLANES
ROW_PARTS = 4
LOG2E = 1.4426950408889634

NT_DIMS = (((1,), (1,)), ((), ()))
TN_DIMS = (((0,), (0,)), ((), ()))


def _params(*semantics):
    return pltpu.CompilerParams(dimension_semantics=semantics,
                                vmem_limit_bytes=VMEM_LIMIT_BYTES)


def _silu(v):
    half = 0.5 * v
    return half + half * jnp.tanh(half)


def _softplus(v):
    return jnp.maximum(v, 0.0) + jnp.log1p(jnp.exp(-jnp.abs(v)))


def _rms_rows(v, w):
    ms = jnp.mean(v * v, axis=-1, keepdims=True)
    return v * lax.rsqrt(ms + EPS) * w


SPLIT_PIECES = 3


def _bf16_pieces(v, n):
    parts, rest = [], v
    for _ in range(n):
        part = rest.astype(BF16)
        parts.append(part)
        rest = rest - part.astype(F32)
    return parts


def _scan_matrix(rows):
    tri = jnp.arange(rows)[None, :] <= jnp.arange(rows)[:, None]
    return jnp.concatenate([tri] * SPLIT_PIECES, axis=1).astype(BF16)


def _cumsum_rows(v, scan_ref):
    stacked = jnp.concatenate(_bf16_pieces(v, SPLIT_PIECES), axis=0)
    return jnp.dot(scan_ref[...], stacked, preferred_element_type=F32)


def _tile(n, pref):
    t = min(n, pref)
    assert n % t == 0, (n, pref)
    return t


def _inproj_kernel(x_ref, nw_ref, w_ref, wdt_ref, o_ref, dt_ref, xn_ref):
    j = pl.program_id(1)

    @pl.when(j == 0)
    def _():
        w_dt = wdt_ref[...].astype(BF16)
        rs = x_ref.shape[0] // ROW_PARTS
        for part in range(ROW_PARTS):
            rows = slice(part * rs, (part + 1) * rs)
            xn = _rms_rows(x_ref[rows, :], nw_ref[...]).astype(BF16)
            xn_ref[rows, :] = xn
            dt_ref[rows, :] = jnp.dot(xn, w_dt, preferred_element_type=F32)
            o_ref[rows, :] = jnp.dot(xn, w_ref[...], preferred_element_type=F32).astype(o_ref.dtype)

    @pl.when(j > 0)
    def _():
        o_ref[...] = jnp.dot(xn_ref[...], w_ref[...],
                             preferred_element_type=F32).astype(o_ref.dtype)


def _in_proj(h, norm_w, w_main, w_dt, tm=1024, tn=1024):
    t, d = h.shape
    n = w_main.shape[1]
    tm, tn = _tile(t, tm), _tile(n, tn)
    return pl.pallas_call(
        _inproj_kernel,
        grid=(t // tm, n // tn),
        in_specs=[
            pl.BlockSpec((tm, d), lambda i, j: (i, 0)),
            pl.BlockSpec((1, d), lambda i, j: (0, 0)),
            pl.BlockSpec((d, tn), lambda i, j: (0, j)),
            pl.BlockSpec((d, LANES), lambda i, j: (0, 0)),
        ],
        out_specs=[
            pl.BlockSpec((tm, tn), lambda i, j: (i, j)),
            pl.BlockSpec((tm, LANES), lambda i, j: (i, 0)),
        ],
        out_shape=[jax.ShapeDtypeStruct((t, n), BF16),
                   jax.ShapeDtypeStruct((t, LANES), F32)],
        scratch_shapes=[pltpu.VMEM((tm, d), BF16)],
        compiler_params=_params("parallel", "arbitrary"),
        name="in_proj",
    )(h, norm_w.reshape(1, d), w_main, w_dt)


def _ssd_kernel(z_ref, x_ref, bc_ref, dt_ref, cwx_ref, cwbc_ref, cbx_ref, cbbc_ref,
                dtb_ref, alog_ref, dskip_ref, gnw_ref, expand_ref, tri_ref, shift_ref,
                y_ref,
                state_ref, xext_ref, bcext_ref, bcs_ref):
    L = SSD_CHUNK
    P = SSM_HEAD_DIM
    N = D_STATE
    G = SSM_GROUPS
    d_inner = x_ref.shape[1]
    d_bc = bc_ref.shape[1]
    heads_per_group = d_inner // (G * P)
    gw = heads_per_group * P
    halo = CONV_HALO

    @pl.when(pl.program_id(1) == 0)
    def _():
        state_ref[...] = jnp.zeros_like(state_ref)
        xext_ref[0:halo, :] = jnp.zeros((halo, d_inner), BF16)
        bcext_ref[0:halo, :] = jnp.zeros((halo, d_bc), BF16)

    xext_ref[halo:halo + L, :] = x_ref[...]
    bcext_ref[halo:halo + L, :] = bc_ref[...]

    def conv_silu(ext_ref, cur_ref, w_ref, b_ref, cols):
        shifted = jnp.dot(shift_ref[...], ext_ref[:, cols], preferred_element_type=F32)
        last = CONV_WIDTH - 1
        acc = 0.5 * b_ref[:, cols] + (0.5 * w_ref[last:last + 1, cols]) * cur_ref[:, cols].astype(F32)
        for k in range(last):
            acc = acc + (0.5 * w_ref[k:k + 1, cols]) * shifted[k * L:(k + 1) * L, :]
        return acc + acc * jnp.tanh(acc)

    for j in range(d_bc // gw):
        cols = slice(j * gw, (j + 1) * gw)
        bcs_ref[:, cols] = conv_silu(bcext_ref, bc_ref, cwbc_ref, cbbc_ref, cols).astype(BF16)
    bcext_ref[0:halo, :] = bcext_ref[L:L + halo, :]

    dtv = _softplus(dt_ref[...] + dtb_ref[...])
    a = dtv * (-LOG2E * jnp.exp(alog_ref[...]))
    acum = _cumsum_rows(a, tri_ref)
    a_last = acum[L - 1:L, :]
    acum_t = acum.T

    def split(v):
        return jnp.concatenate(_bf16_pieces(v, 2), axis=1)

    dt_hl = split(dtv)
    e_out_hl = split(jnp.exp2(acum))
    e_in_hl = split(dtv * jnp.exp2(a_last - acum))

    def expand(hl, cols):
        return jnp.dot(hl, expand_ref[:, cols], preferred_element_type=F32)

    row_id = lax.broadcasted_iota(jnp.int32, (L, L), 0)
    col_id = lax.broadcasted_iota(jnp.int32, (L, L), 1)
    causal = col_id <= row_id
    lane_id = lax.broadcasted_iota(jnp.int32, (L, 2 * P), 1)
    first_head = lane_id < P

    for g in range(G):
        gs = slice(g * gw, (g + 1) * gw)
        xs = conv_silu(xext_ref, x_ref, cwx_ref, cbx_ref, gs)
        xdt = (xs * expand(dt_hl, gs)).astype(BF16)
        xw = (xs * expand(e_in_hl, gs)).astype(BF16)
        e_out_x = expand(e_out_hl, gs)
        b_g = bcs_ref[:, g * N:(g + 1) * N]
        c_g = bcs_ref[:, G * N + g * N:G * N + (g + 1) * N]
        cb = lax.dot_general(c_g, b_g, NT_DIMS, preferred_element_type=F32)
        s_g = state_ref[:, gs]
        y_off = jnp.dot(c_g, s_g.astype(BF16), preferred_element_type=F32) * e_out_x
        upd = lax.dot_general(b_g, xw, TN_DIMS, preferred_element_type=F32)
        state_ref[:, gs] = s_g * e_out_x[L - 1:L, :] + upd
        y_pairs = []
        for pr in range(heads_per_group // 2):
            h0 = g * heads_per_group + 2 * pr
            ms = []
            for hh in (h0, h0 + 1):
                seg = acum[:, hh:hh + 1] - acum_t[hh:hh + 1, :]
                decay = jnp.exp2(jnp.where(causal, seg, -jnp.inf))
                ms.append((cb * decay).astype(BF16))
            m_cat = jnp.concatenate(ms, axis=1)
            xp = xdt[:, pr * 2 * P:(pr + 1) * 2 * P]
            zero = jnp.zeros_like(xp)
            rhs = jnp.concatenate([jnp.where(first_head, xp, zero),
                                   jnp.where(first_head, zero, xp)], axis=0)
            y_pairs.append(jnp.dot(m_cat, rhs, preferred_element_type=F32))
        y = jnp.concatenate(y_pairs, axis=1) + y_off + xs * dskip_ref[:, gs]
        y = y * _silu(z_ref[:, gs].astype(F32))
        y_ref[:, gs] = _rms_rows(y, gnw_ref[:, gs]).astype(y_ref.dtype)
    xext_ref[0:halo, :] = xext_ref[L:L + halo, :]


def _ssd(zx, dt_raw, conv_w, conv_b, dt_bias, a_log, d_skip, gnorm_w, bsz, seq):
    t = zx.shape[0]
    L, P, N, G = SSD_CHUNK, SSM_HEAD_DIM, D_STATE, SSM_GROUPS
    n_heads = dt_bias.shape[0]
    d_inner = n_heads * P
    d_bc = 2 * G * N
    assert zx.shape[1] == 2 * d_inner + d_bc and n_heads <= LANES
    assert d_inner % d_bc == 0 and seq % L == 0
    nc = seq // L
    bc_blk = 2 * d_inner // d_bc

    pad_h = LANES - n_heads
    dtb = jnp.pad(dt_bias.astype(F32), (0, pad_h)).reshape(1, LANES)
    alog = jnp.pad(a_log.astype(F32), (0, pad_h)).reshape(1, LANES)
    dskip_x = jnp.repeat(d_skip.astype(F32), P).reshape(1, d_inner)
    head_of_channel = jnp.arange(d_inner, dtype=jnp.int32) // P
    sel = (jnp.arange(LANES, dtype=jnp.int32)[:, None] == head_of_channel[None, :])
    expand_mat = jnp.concatenate([sel, sel], axis=0).astype(BF16)
    tri = _scan_matrix(L)
    n_shift = CONV_WIDTH - 1
    src_row = (CONV_HALO - n_shift + jnp.arange(n_shift)[:, None] + jnp.arange(L)[None, :]).reshape(-1)
    shift_mat = (src_row[:, None] == jnp.arange(CONV_HALO + L)[None, :]).astype(BF16)

    row = lambda b, c: b * nc + c
    const = lambda b, c: (0, 0)
    return pl.pallas_call(
        _ssd_kernel,
        grid=(bsz, nc),
        in_specs=[
            pl.BlockSpec((L, d_inner), lambda b, c: (row(b, c), 0)),
            pl.BlockSpec((L, d_inner), lambda b, c: (row(b, c), 1)),
            pl.BlockSpec((L, d_bc), lambda b, c: (row(b, c), bc_blk)),
            pl.BlockSpec((L, LANES), lambda b, c: (row(b, c), 0)),
            pl.BlockSpec((CONV_WIDTH, d_inner), const),
            pl.BlockSpec((CONV_WIDTH, d_bc), const),
            pl.BlockSpec((1, d_inner), const),
            pl.BlockSpec((1, d_bc), const),
            pl.BlockSpec((1, LANES), const),
            pl.BlockSpec((1, LANES), const),
            pl.BlockSpec((1, d_inner), const),
            pl.BlockSpec((1, d_inner), const),
            pl.BlockSpec((2 * LANES, d_inner), const),
            pl.BlockSpec(tri.shape, const),
            pl.BlockSpec((n_shift * L, CONV_HALO + L), const),
        ],
        out_specs=pl.BlockSpec((L, d_inner), lambda b, c: (row(b, c), 0)),
        out_shape=jax.ShapeDtypeStruct((t, d_inner), BF16),
        scratch_shapes=[
            pltpu.VMEM((N, d_inner), F32),
            pltpu.VMEM((CONV_HALO + L, d_inner), BF16),
            pltpu.VMEM((CONV_HALO + L, d_bc), BF16),
            pltpu.VMEM((L, d_bc), BF16),
        ],
        compiler_params=_params("parallel", "arbitrary"),
        name="ssd",
    )(zx, zx, zx, dt_raw,
      conv_w[:, :d_inner], conv_w[:, d_inner:],
      conv_b[:d_inner].reshape(1, d_inner), conv_b[d_inner:].reshape(1, d_bc),
      dtb, alog, dskip_x, gnorm_w.reshape(1, d_inner), expand_mat, tri, shift_mat)


def _mm_res_kernel(a_ref, w_ref, r_ref, h_ref, *norm_refs):
    h = r_ref[...] + jnp.dot(a_ref[...], w_ref[...], preferred_element_type=F32)
    h_ref[...] = h
    if norm_refs:
        hb_ref, rinv_ref = norm_refs
        hb_ref[...] = h.astype(BF16)
        ms = jnp.mean(h * h, axis=-1, keepdims=True)
        rinv_ref[...] = jnp.broadcast_to(lax.rsqrt(ms + EPS), rinv_ref.shape)


MM_RES_ROW_CHOICES = (512, 256)
MM_RES_VMEM_BUDGET = 50 * 1024 * 1024


def _mm_res_rows(t, k, n, emit_norm):
    for tm in MM_RES_ROW_CHOICES:
        tiles = tm * k * 2 + 2 * tm * n * 4 + (tm * n * 2 + tm * LANES * 4 if emit_norm else 0)
        if t % tm == 0 and k * n * 2 + 2 * tiles <= MM_RES_VMEM_BUDGET:
            return tm
    return _tile(t, MM_RES_ROW_CHOICES[-1])


def _mm_res(a, w_layers, layer, res, emit_norm, name="mm_res"):
    t, k = a.shape
    n = w_layers.shape[2]
    tm = _mm_res_rows(t, k, n, emit_norm)
    out_specs = [pl.BlockSpec((tm, n), lambda i: (i, 0))]
    out_shape = [jax.ShapeDtypeStruct((t, n), F32)]
    if emit_norm:
        out_specs += [pl.BlockSpec((tm, n), lambda i: (i, 0)),
                      pl.BlockSpec((tm, LANES), lambda i: (i, 0))]
        out_shape += [jax.ShapeDtypeStruct((t, n), BF16),
                      jax.ShapeDtypeStruct((t, LANES), F32)]
    return pl.pallas_call(
        _mm_res_kernel,
        grid=(t // tm,),
        in_specs=[
            pl.BlockSpec((tm, k), lambda i: (i, 0)),
            pl.BlockSpec((None, k, n), lambda i: (layer, 0, 0), pipeline_mode=pl.Buffered(1)),
            pl.BlockSpec((tm, n), lambda i: (i, 0)),
        ],
        out_specs=out_specs,
        out_shape=out_shape,
        compiler_params=_params("arbitrary"),
        name=name,
    )(a, w_layers, res)


def _stage_weight(w_ref, nw_ref, wb_ref):
    wb_ref[...] = (w_ref[...] * nw_ref[...]).astype(BF16)


def _scale_rows(acc, rinv):
    return jnp.concatenate([acc[:, c * LANES:(c + 1) * LANES] * rinv
                            for c in range(acc.shape[1] // LANES)], axis=1)


def _projected_parts(hb_ref, rinv_ref, wb_ref):
    rs = hb_ref.shape[0] // ROW_PARTS
    parts = [slice(p * rs, (p + 1) * rs) for p in range(ROW_PARTS)]
    accs = [jnp.dot(hb_ref[rows, :], wb_ref[...], preferred_element_type=F32) for rows in parts]
    return [(rows, _scale_rows(acc, rinv_ref[rows, :])) for rows, acc in zip(parts, accs)]


def _ffn_up_kernel(hb_ref, rinv_ref, nw_ref, wg_ref, wu_ref, o_ref, wgb_ref, wub_ref):
    @pl.when(pl.program_id(1) == 0)
    def _():
        _stage_weight(wg_ref, nw_ref, wgb_ref)
        _stage_weight(wu_ref, nw_ref, wub_ref)

    hb = hb_ref[...]
    rinv = rinv_ref[...]
    g = _scale_rows(jnp.dot(hb, wgb_ref[...], preferred_element_type=F32), rinv)
    u = _scale_rows(jnp.dot(hb, wub_ref[...], preferred_element_type=F32), rinv)
    o_ref[...] = (_silu(g) * u).astype(o_ref.dtype)


def _ffn_up(hb, rinv, norm_w, w_gate_up, layer, tm=1024, tn=512):
    t, d = hb.shape
    d_ff = w_gate_up.shape[2] // 2
    tm, tn = _tile(t, tm), _tile(d_ff, tn)
    nj = d_ff // tn
    return pl.pallas_call(
        _ffn_up_kernel,
        grid=(nj, t // tm),
        in_specs=[
            pl.BlockSpec((tm, d), lambda j, i: (i, 0)),
            pl.BlockSpec((tm, LANES), lambda j, i: (i, 0)),
            pl.BlockSpec((d, 1), lambda j, i: (0, 0)),
            pl.BlockSpec((None, d, tn), lambda j, i: (layer, 0, j)),
            pl.BlockSpec((None, d, tn), lambda j, i: (layer, 0, j + nj)),
        ],
        out_specs=pl.BlockSpec((tm, tn), lambda j, i: (i, j)),
        out_shape=jax.ShapeDtypeStruct((t, d_ff), BF16),
        scratch_shapes=[pltpu.VMEM((d, tn), BF16), pltpu.VMEM((d, tn), BF16)],
        compiler_params=_params("arbitrary", "arbitrary"),
        name="ffn_up",
    )(hb, rinv, norm_w.reshape(d, 1), w_gate_up, w_gate_up)


def _ffn(h, hb, rinv, norm_w, w_gate_up, w_down_b, layer, emit_norm):
    act = _ffn_up(hb, rinv, norm_w, w_gate_up, layer)
    return _mm_res(act, w_down_b, layer, h, emit_norm, name="ffn_down")


def _head_norm(acc, hw, scale):
    parts = []
    for c in range(acc.shape[1] // ATT_HEAD_DIM):
        part = acc[:, c * ATT_HEAD_DIM:(c + 1) * ATT_HEAD_DIM]
        parts.append(_rms_rows(part, hw) * scale)
    return jnp.concatenate(parts, axis=1)


def _qproj_kernel(hb_ref, rinv_ref, nw_ref, w_ref, hw_ref, o_ref, wb_ref, *, scale):
    @pl.when(pl.program_id(1) == 0)
    def _():
        _stage_weight(w_ref, nw_ref, wb_ref)

    for rows, acc in _projected_parts(hb_ref, rinv_ref, wb_ref):
        o_ref[:, rows] = _head_norm(acc, hw_ref[...], scale).T.astype(o_ref.dtype)


def _q_proj(hb, rinv, norm_w, w_layers, layer, head_w, scale, tm=1024, tn=1024):
    t, d = hb.shape
    n = w_layers.shape[2]
    tm, tn = _tile(t, tm), _tile(n, tn)
    return pl.pallas_call(
        functools.partial(_qproj_kernel, scale=scale),
        grid=(n // tn, t // tm),
        in_specs=[
            pl.BlockSpec((tm, d), lambda j, i: (i, 0)),
            pl.BlockSpec((tm, LANES), lambda j, i: (i, 0)),
            pl.BlockSpec((d, 1), lambda j, i: (0, 0)),
            pl.BlockSpec((None, d, tn), lambda j, i: (layer, 0, j)),
            pl.BlockSpec((1, ATT_HEAD_DIM), lambda j, i: (0, 0)),
        ],
        out_specs=pl.BlockSpec((tn, tm), lambda j, i: (j, i)),
        out_shape=jax.ShapeDtypeStruct((n, t), BF16),
        scratch_shapes=[pltpu.VMEM((d, tn), BF16)],
        compiler_params=_params("arbitrary", "arbitrary"),
        name="q_proj",
    )(hb, rinv, norm_w.reshape(d, 1), w_layers, head_w.reshape(1, ATT_HEAD_DIM))


def _kvproj_kernel(hb_ref, rinv_ref, nw_ref, w_ref, wf_ref, bf_ref, hw_ref, tri_ref,
                   k_ref, vt_ref, cum_ref, wb_ref, wfb_ref, carry_ref, *, n_k_blocks, blocks_per_seq):
    j = pl.program_id(0)
    i = pl.program_id(1)
    tm = hb_ref.shape[0]

    @pl.when(i == 0)
    def _():
        _stage_weight(w_ref, nw_ref, wb_ref)
        _stage_weight(wf_ref, nw_ref, wfb_ref)
        carry_ref[...] = jnp.zeros_like(carry_ref)

    def write_keys(parts):
        for rows, acc in parts:
            k_ref[rows, :] = _head_norm(acc, hw_ref[...], 1.0).astype(k_ref.dtype)

    def gates():
        logit = jnp.dot(hb_ref[...], wfb_ref[...], preferred_element_type=F32) * rinv_ref[...]
        log_f = -_softplus(-(logit + bf_ref[...]))
        rows_per_scan = tri_ref.shape[0]
        carry = jnp.where(i % blocks_per_seq == 0, 0.0, carry_ref[0:1, :])
        for sb in range(tm // rows_per_scan):
            rows = slice(sb * rows_per_scan, (sb + 1) * rows_per_scan)
            cum = _cumsum_rows(log_f[rows, :], tri_ref) + carry
            cum_ref[rows, :] = jnp.concatenate(_bf16_pieces(-LOG2E * cum, BIAS_PIECES), axis=1)
            carry = cum[rows_per_scan - 1:rows_per_scan, :]
        carry_ref[...] = jnp.broadcast_to(carry, carry_ref.shape)

    @pl.when(j == 0)
    def _():
        parts = _projected_parts(hb_ref, rinv_ref, wb_ref)
        gates()
        write_keys(parts)

    @pl.when(jnp.logical_and(j > 0, j < n_k_blocks))
    def _():
        write_keys(_projected_parts(hb_ref, rinv_ref, wb_ref))

    @pl.when(j >= n_k_blocks)
    def _():
        for rows, acc in _projected_parts(hb_ref, rinv_ref, wb_ref):
            vt_ref[:, rows] = acc.T.astype(vt_ref.dtype)


def _kv_proj(hb, rinv, norm_w, w_kvf, w_f, b_f, head_w, d_att, seq, tm=1024, tn=1024, scan_rows=256):
    t, d = hb.shape
    n = 2 * d_att
    tm, tn = _tile(seq, tm), _tile(d_att, tn)
    ni = t // tm
    nk = d_att // tn
    tri = _scan_matrix(_tile(tm, scan_rows))
    kern = functools.partial(_kvproj_kernel, n_k_blocks=nk, blocks_per_seq=seq // tm)
    return pl.pallas_call(
        kern,
        grid=(n // tn, ni),
        in_specs=[
            pl.BlockSpec((tm, d), lambda j, i: (i, 0)),
            pl.BlockSpec((tm, LANES), lambda j, i: (i, 0)),
            pl.BlockSpec((d, 1), lambda j, i: (0, 0)),
            pl.BlockSpec((d, tn), lambda j, i: (0, j)),
            pl.BlockSpec((d, LANES), lambda j, i: (0, 0)),
            pl.BlockSpec((1, LANES), lambda j, i: (0, 0)),
            pl.BlockSpec((1, ATT_HEAD_DIM), lambda j, i: (0, 0)),
            pl.BlockSpec(tri.shape, lambda j, i: (0, 0)),
        ],
        out_specs=[
            pl.BlockSpec((tm, tn), lambda j, i: (jnp.where(j < nk, i, ni - 1), jnp.minimum(j, nk - 1))),
            pl.BlockSpec((tn, tm), lambda j, i: (jnp.maximum(j - nk, 0), jnp.where(j < nk, 0, i))),
            pl.BlockSpec((tm, BIAS_PIECES * LANES), lambda j, i: (jnp.where(j == 0, i, ni - 1), 0)),
        ],
        out_shape=[jax.ShapeDtypeStruct((t, d_att), BF16),
                   jax.ShapeDtypeStruct((d_att, t), BF16),
                   jax.ShapeDtypeStruct((t, BIAS_PIECES * LANES), BF16)],
        scratch_shapes=[pltpu.VMEM((d, tn), BF16), pltpu.VMEM((d, LANES), BF16),
                        pltpu.VMEM((SUBLANES, LANES), F32)],
        compiler_params=_params("arbitrary", "arbitrary"),
        name="kv_proj",
    )(hb, rinv, norm_w.reshape(d, 1), w_kvf, w_f, b_f, head_w.reshape(1, ATT_HEAD_DIM), tri)


BIAS_PIECES = SPLIT_PIECES
SCORE_LOOKAHEAD = 1


def _attn_kernel(qt_ref, k_ref, vt_ref, cum_ref, o_ref, kb_ref, *, tq):
    seq, hd = k_ref.shape
    nq = seq // tq
    head = pl.program_id(1)
    sub = lax.broadcasted_iota(jnp.int32, (hd, 1), 0)

    def ones_rows(width):
        return jnp.broadcast_to(jnp.where(sub < BIAS_PIECES, 1.0, 0.0), (hd, width)).astype(BF16)

    @pl.when(head == 0)
    def _():
        n_heads = kb_ref.shape[0]
        src = lax.broadcasted_iota(jnp.int32, (cum_ref.shape[1], n_heads * hd), 0)
        dst = lax.broadcasted_iota(jnp.int32, (cum_ref.shape[1], n_heads * hd), 1)
        dst_head, dst_lane = dst // hd, dst % hd
        pick = jnp.where((src == dst_lane * LANES + dst_head) & (dst_lane < BIAS_PIECES),
                         1.0, 0.0).astype(BF16)

        def prepare_rows(r, carry):
            rows = pl.ds(pl.multiple_of(r * tq, tq), tq)
            placed = jnp.dot(cum_ref[rows, :], pick, preferred_element_type=F32).astype(BF16)
            for hh in range(n_heads):
                kb_ref[hh, rows, :] = placed[:, hh * hd:(hh + 1) * hd]
            return carry

        lax.fori_loop(0, nq, prepare_rows, 0)

    tile_row = lax.broadcasted_iota(jnp.int32, (BF16_SUBLANES, 1), 0)

    def denom_rows(width):
        return jnp.broadcast_to(jnp.where(tile_row == 0, 1.0, 0.0), (BF16_SUBLANES, width)).astype(BF16)

    half = tq // 2
    def visible(n_keys, first_query):
        key_id = lax.broadcasted_iota(jnp.int32, (n_keys, half), 0)
        qry_id = lax.broadcasted_iota(jnp.int32, (n_keys, half), 1)
        return key_id <= qry_id + first_query

    diagonal_parts = ((half, slice(0, half), visible(half, 0)),
                      (tq, slice(half, tq), visible(tq, half)))
    full_parts = ((tq, slice(0, tq), None),)

    def scores(q0, start, parts):
        out = []
        for n_keys, lanes, _ in parts:
            keys = pl.ds(start, n_keys)
            width = lanes.stop - lanes.start
            k_aug = jnp.concatenate([k_ref[keys, :], kb_ref[head, keys, :]], axis=1)
            qt = jnp.concatenate([qt_ref[:, q0 + lanes.start:q0 + lanes.stop], ones_rows(width)],
                                 axis=0)
            out.append(jnp.dot(k_aug, qt, preferred_element_type=F32))
        return out

    def update(s_parts, start, carry, parts):
        m, acc = carry if carry is not None else (None, None)
        m_out, acc_out = [], []
        for s, (n_keys, lanes, mask) in zip(s_parts, parts):
            if mask is not None:
                s = jnp.where(mask, s, -jnp.inf)
            width = lanes.stop - lanes.start
            if carry is None:
                m_old, acc_old = jnp.full((SUBLANES, width), -jnp.inf, F32), None
            else:
                m_old, acc_old = m[:, lanes], acc[:, lanes]
            m_new = jnp.maximum(m_old, jnp.max(s, axis=0, keepdims=True))
            alpha = jnp.exp2(m_old - m_new)[0:1, :]
            p = jnp.exp2(s - m_new[0:1, :]).astype(BF16)
            vt_aug = jnp.concatenate([vt_ref[:, pl.ds(start, n_keys)], denom_rows(n_keys)], axis=0)
            pv = jnp.dot(vt_aug, p, preferred_element_type=F32)
            acc_out.append(pv if acc_old is None else alpha * acc_old + pv)
            m_out.append(m_new)
        return jnp.concatenate(m_out, axis=1), jnp.concatenate(acc_out, axis=1)

    work = []
    lo, hi = 0, nq - 1
    while lo <= hi:
        chains = [[(q, kb) for kb in range(q + 1)] for q in sorted({lo, hi})]
        while any(chains):
            for chain in chains:
                if chain:
                    work.append(chain.pop(0))
        lo, hi = lo + 1, hi - 1

    def issue(item):
        qi, kb = item
        parts = diagonal_parts if kb == qi else full_parts
        return scores(qi * tq, kb * tq, parts), parts

    pending = [issue(item) for item in work[:SCORE_LOOKAHEAD]]
    carries = {}
    for idx, (qi, kb) in enumerate(work):
        s_cur, parts = pending.pop(0)
        if idx + SCORE_LOOKAHEAD < len(work):
            pending.append(issue(work[idx + SCORE_LOOKAHEAD]))
        carries[qi] = update(s_cur, kb * tq, carries.get(qi), parts)
        if kb == qi:
            _, acc = carries.pop(qi)
            o_ref[qi * tq:(qi + 1) * tq, :] = (acc[0:hd, :] / acc[hd:hd + 1, :]).T.astype(o_ref.dtype)


def _attention(q_t, k, v_t, cum, bsz, seq, n_heads, tq=512):
    t = k.shape[0]
    tq = _tile(seq, tq)
    hd = ATT_HEAD_DIM
    return pl.pallas_call(
        functools.partial(_attn_kernel, tq=tq),
        grid=(bsz, n_heads),
        in_specs=[
            pl.BlockSpec((hd, seq), lambda b, h: (h, b)),
            pl.BlockSpec((seq, hd), lambda b, h: (b, h)),
            pl.BlockSpec((hd, seq), lambda b, h: (h, b)),
            pl.BlockSpec((seq, cum.shape[1]), lambda b, h: (b, 0)),
        ],
        out_specs=pl.BlockSpec((seq, hd), lambda b, h: (b, h)),
        out_shape=jax.ShapeDtypeStruct((t, n_heads * hd), BF16),
        scratch_shapes=[pltpu.VMEM((n_heads, seq, hd), BF16)],
        compiler_params=_params("arbitrary", "arbitrary"),
        name="fox_attention",
    )(q_t, k, v_t, cum)


def kernel(x, a_norm_w, a_in_proj, a_conv_w, a_conv_b, a_dt_bias, a_A_log, a_D, a_gnorm_w, a_out_proj, kv_norm_w, w_kvf, b_f, k_norm_w, b_norm_w, w_q, q_norm_w, w_o, ffn_norm_w, w_gate_up, w_down):
    bsz, seq, d_model = x.shape
    t = bsz * seq
    n_a = a_norm_w.shape[0]
    n_b = b_norm_w.shape[0]
    depth = n_a + n_b
    assert n_a >= 1, "the attention layers take the bf16 stream a Mamba-2 layer emits"
    h = x.reshape(t, d_model).astype(F32)
    hb = rinv = None
    w_out_b = a_out_proj.astype(BF16)
    w_down_b = w_down.astype(BF16)
    w_o_b = w_o.astype(BF16)

    for i in range(n_a):
        n_ssm_heads = a_dt_bias.shape[1]
        n_main = a_in_proj.shape[2] - n_ssm_heads
        w_main = a_in_proj[i, :, :n_main].astype(BF16)
        w_dt = jnp.pad(a_in_proj[i, :, n_main:], ((0, 0), (0, LANES - n_ssm_heads)))
        zx, dt_raw = _in_proj(h, a_norm_w[i], w_main, w_dt)
        y = _ssd(zx, dt_raw, a_conv_w[i], a_conv_b[i], a_dt_bias[i], a_A_log[i], a_D[i],
                 a_gnorm_w[i], bsz, seq)
        h, hb, rinv = _mm_res(y, w_out_b, i, h, True, name="out_proj")
        more = i + 1 < depth
        h, *norm = _ffn(h, hb, rinv, ffn_norm_w[i], w_gate_up, w_down_b, i, more)
        hb, rinv = norm if more else (None, None)

    n_att_heads = b_f.shape[0]
    d_att = n_att_heads * ATT_HEAD_DIM
    scale = ATT_HEAD_DIM ** -0.5 * LOG2E
    k = v_t = cum = None
    for j in range(n_b):
        if j == 0:
            w_f = jnp.pad(w_kvf[:, 2 * d_att:], ((0, 0), (0, LANES - n_att_heads)))
            b_pad = jnp.pad(b_f.astype(F32), (0, LANES - n_att_heads)).reshape(1, LANES)
            k, v_t, cum = _kv_proj(hb, rinv, kv_norm_w, w_kvf, w_f, b_pad, k_norm_w, d_att, seq)
        q_t = _q_proj(hb, rinv, b_norm_w[j], w_q, j, q_norm_w[j], scale)
        o = _attention(q_t, k, v_t, cum, bsz, seq, n_att_heads)
        h, hb, rinv = _mm_res(o, w_o_b, j, h, True, name="attn_out")
        more = n_a + j + 1 < depth
        h, *norm = _ffn(h, hb, rinv, ffn_norm_w[n_a + j], w_gate_up, w_down_b, n_a + j, more)
        hb, rinv = norm if more else (None, None)

    return h.reshape(bsz, seq, d_model).astype(x.dtype)
```
